```python
import math
import jax, jax.numpy as jnp
from jax import lax
import numpy as np

D_MODEL = 1024
BATCH = 8
SEQ = 2048
DEPTH = 2

CHUNK = 64
N_MIXERS = 2
N_ATTN_LAYERS = (DEPTH + 1) // 2
N_GMLP_LAYERS = DEPTH // 2

N_HEADS = 8
HEAD_DIM_QK = D_MODEL // (2 * N_HEADS)
HEAD_DIM_V = 2 * HEAD_DIM_QK
QK_WIDTH = N_HEADS * 2 * HEAD_DIM_QK
V_WIDTH = N_HEADS * HEAD_DIM_V
Q_BLOCK = 128

GMLP_BLOCK = 128
GMLP_HALF = 2 * D_MODEL
GMLP_GROUPS = 8
GMLP_GROUP_DIM = GMLP_HALF // GMLP_GROUPS

D_FF = -(-8 * D_MODEL // (3 * 256)) * 256

DEEPNORM_ALPHA = (2 * DEPTH) ** 0.25
DEEPNORM_BETA = (8 * DEPTH) ** -0.25
LN_EPS = 1e-5
MASK_VALUE = -1e30

kernel_name = "hybrid_diffattn_gmlp_deepnorm"


def _layernorm(x, g, b):
    xf = x.astype(jnp.float32)
    mu = jnp.mean(xf, axis=-1, keepdims=True)
    var = jnp.mean(jnp.square(xf - mu), axis=-1, keepdims=True)
    y = (xf - mu) * lax.rsqrt(var + LN_EPS) * g.astype(jnp.float32) + b.astype(jnp.float32)
    return y.astype(x.dtype)


def _rmsnorm(x, g):
    xf = x.astype(jnp.float32)
    y = xf * lax.rsqrt(jnp.mean(jnp.square(xf), axis=-1, keepdims=True) + LN_EPS) * g.astype(jnp.float32)
    return y.astype(x.dtype)


def _lambda_init(layer_idx):
    return 0.8 - 0.6 * math.exp(-0.3 * layer_idx)


def _diff_attention(h, w_qkv, lq1, lk1, lq2, lk2, g_sub, w_o, lambda_init, slopes, pos):
    B, S, _ = h.shape
    qkv = h @ w_qkv
    q, k, v = jnp.split(qkv, [QK_WIDTH, 2 * QK_WIDTH], axis=-1)
    q = q.reshape(B, S, N_HEADS, 2, HEAD_DIM_QK)
    k = k.reshape(B, S, N_HEADS, 2, HEAD_DIM_QK)
    v = v.reshape(B, S, N_HEADS, HEAD_DIM_V)
    lam = (jnp.exp(jnp.sum(lq1.astype(jnp.float32) * lk1.astype(jnp.float32)))
           - jnp.exp(jnp.sum(lq2.astype(jnp.float32) * lk2.astype(jnp.float32)))
           + lambda_init)
    scale = HEAD_DIM_QK ** -0.5
    n_blk = S // Q_BLOCK
    q_blocks = q.reshape(B, n_blk, Q_BLOCK, N_HEADS, 2, HEAD_DIM_QK).transpose(1, 0, 2, 3, 4, 5)
    qpos_blocks = pos.reshape(n_blk, Q_BLOCK)
    k_chunk = pos // CHUNK

    def one_block(args):
        q_blk, qpos = args
        s = jnp.einsum('bqhcd,bkhcd->bhcqk', q_blk, k).astype(jnp.float32) * scale
        dist = jnp.abs(qpos[:, None] - pos[None, :]).astype(jnp.float32)
        s = s - slopes[None, :, None, None, None] * dist[None, None, None]
        allowed = k_chunk[None, :] <= (qpos // CHUNK)[:, None]
        s = jnp.where(allowed[None, None, None], s, MASK_VALUE)
        p = jax.nn.softmax(s, axis=-1)
        a = (p[:, :, 0] - lam * p[:, :, 1]).astype(v.dtype)
        return jnp.einsum('bhqk,bkhd->bqhd', a, v)

    o = lax.map(one_block, (q_blocks, qpos_blocks))
    o = o.transpose(1, 0, 2, 3, 4).reshape(B, S, N_HEADS, HEAD_DIM_V)
    o = _rmsnorm(o, g_sub) * (1.0 - lambda_init)
    return o.reshape(B, S, V_WIDTH) @ w_o


def _gmlp_spatial_gating(h, w_in, b_in, ln_g, ln_b, w_s, b_s, w_out):
    B, S, _ = h.shape
    z = jax.nn.gelu(h @ w_in + b_in, approximate=False)
    u, v = jnp.split(z, 2, axis=-1)
    v = _layernorm(v, ln_g, ln_b)
    v = v.reshape(B, S // GMLP_BLOCK, GMLP_BLOCK, GMLP_GROUPS, GMLP_GROUP_DIM)
    tri = jnp.tril(jnp.ones((GMLP_BLOCK, GMLP_BLOCK), dtype=w_s.dtype))
    v = jnp.einsum('gts,bnsgc->bntgc', w_s * tri[None], v) + b_s.T[None, None, :, :, None]
    y = u * v.reshape(B, S, GMLP_HALF)
    return y @ w_out


def _swiglu(h, w_gate, w_up, w_down):
    return (jax.nn.silu(h @ w_gate) * (h @ w_up)) @ w_down


def setup_inputs(seed: int = 0) -> dict:
    key = jax.random.key(seed)
    ks = jax.random.split(key, 24)
    f32 = jnp.float32

    def nrm(k, shape, scale):
        return jax.random.normal(k, shape, dtype=f32) * scale

    x = nrm(ks[0], (BATCH, SEQ, D_MODEL), 1.0)
    w_qk = nrm(ks[1], (N_ATTN_LAYERS, D_MODEL, 2 * QK_WIDTH), D_MODEL ** -0.5)
    w_v = nrm(ks[2], (N_ATTN_LAYERS, D_MODEL, V_WIDTH), D_MODEL ** -0.5 * DEEPNORM_BETA)
    attn_w_qkv = jnp.concatenate([w_qk, w_v], axis=-1)
    attn_lambda_q1 = nrm(ks[3], (N_ATTN_LAYERS, HEAD_DIM_QK), 0.1)
    attn_lambda_k1 = nrm(ks[4], (N_ATTN_LAYERS, HEAD_DIM_QK), 0.1)
    attn_lambda_q2 = nrm(ks[5], (N_ATTN_LAYERS, HEAD_DIM_QK), 0.1)
    attn_lambda_k2 = nrm(ks[6], (N_ATTN_LAYERS, HEAD_DIM_QK), 0.1)
    attn_subln_g = 1.0 + nrm(ks[7], (N_ATTN_LAYERS, HEAD_DIM_V), 0.02)
    attn_w_o = nrm(ks[8], (N_ATTN_LAYERS, V_WIDTH, D_MODEL), V_WIDTH ** -0.5 * DEEPNORM_BETA)

    gmlp_w_in = nrm(ks[9], (N_GMLP_LAYERS, D_MODEL, 2 * GMLP_HALF), D_MODEL ** -0.5)
    gmlp_b_in = nrm(ks[10], (N_GMLP_LAYERS, 2 * GMLP_HALF), 0.01)
    gmlp_ln_g = 1.0 + nrm(ks[11], (N_GMLP_LAYERS, GMLP_HALF), 0.02)
    gmlp_ln_b = nrm(ks[12], (N_GMLP_LAYERS, GMLP_HALF), 0.01)
    gmlp_w_s = nrm(ks[13], (N_GMLP_LAYERS, GMLP_GROUPS, GMLP_BLOCK, GMLP_BLOCK), GMLP_BLOCK ** -0.5)
    gmlp_b_s = 1.0 + nrm(ks[14], (N_GMLP_LAYERS, GMLP_GROUPS, GMLP_BLOCK), 0.01)
    gmlp_w_out = nrm(ks[15], (N_GMLP_LAYERS, GMLP_HALF, D_MODEL), GMLP_HALF ** -0.5 * DEEPNORM_BETA)

    ln_mix_g = 1.0 + nrm(ks[16], (DEPTH, D_MODEL), 0.02)
    ln_mix_b = nrm(ks[17], (DEPTH, D_MODEL), 0.01)
    ffn_w_gate = nrm(ks[18], (DEPTH, D_MODEL, D_FF), D_MODEL ** -0.5)
    ffn_w_up = nrm(ks[19], (DEPTH, D_MODEL, D_FF), D_MODEL ** -0.5 * DEEPNORM_BETA)
    ffn_w_down = nrm(ks[20], (DEPTH, D_FF, D_MODEL), D_FF ** -0.5 * DEEPNORM_BETA)
    ln_ffn_g = 1.0 + nrm(ks[21], (DEPTH, D_MODEL), 0.02)
    ln_ffn_b = nrm(ks[22], (DEPTH, D_MODEL), 0.01)
    return {
        "x": x,
        "attn_w_qkv": attn_w_qkv,
        "attn_lambda_q1": attn_lambda_q1,
        "attn_lambda_k1": attn_lambda_k1,
        "attn_lambda_q2": attn_lambda_q2,
        "attn_lambda_k2": attn_lambda_k2,
        "attn_subln_g": attn_subln_g,
        "attn_w_o": attn_w_o,
        "gmlp_w_in": gmlp_w_in,
        "gmlp_b_in": gmlp_b_in,
        "gmlp_ln_g": gmlp_ln_g,
        "gmlp_ln_b": gmlp_ln_b,
        "gmlp_w_s": gmlp_w_s,
        "gmlp_b_s": gmlp_b_s,
        "gmlp_w_out": gmlp_w_out,
        "ln_mix_g": ln_mix_g,
        "ln_mix_b": ln_mix_b,
        "ffn_w_gate": ffn_w_gate,
        "ffn_w_up": ffn_w_up,
        "ffn_w_down": ffn_w_down,
        "ln_ffn_g": ln_ffn_g,
        "ln_ffn_b": ln_ffn_b,
    }


def reference(x, attn_w_qkv, attn_lambda_q1, attn_lambda_k1, attn_lambda_q2, attn_lambda_k2,
              attn_subln_g, attn_w_o, gmlp_w_in, gmlp_b_in, gmlp_ln_g, gmlp_ln_b, gmlp_w_s,
              gmlp_b_s, gmlp_w_out, ln_mix_g, ln_mix_b, ffn_w_gate, ffn_w_up, ffn_w_down,
              ln_ffn_g, ln_ffn_b):
    S = x.shape[1]
    pos = jnp.arange(S, dtype=jnp.int32)
    slopes = 2.0 ** (-8.0 * jnp.arange(1, N_HEADS + 1, dtype=jnp.float32) / N_HEADS)
    for i in range(DEPTH):
        j = i // N_MIXERS
        if i % N_MIXERS == 0:
            mix = _diff_attention(x, attn_w_qkv[j], attn_lambda_q1[j], attn_lambda_k1[j],
                                  attn_lambda_q2[j], attn_lambda_k2[j], attn_subln_g[j],
                                  attn_w_o[j], _lambda_init(i), slopes, pos)
        else:
            mix = _gmlp_spatial_gating(x, gmlp_w_in[j], gmlp_b_in[j], gmlp_ln_g[j], gmlp_ln_b[j],
                                       gmlp_w_s[j], gmlp_b_s[j], gmlp_w_out[j])
        x = _layernorm(DEEPNORM_ALPHA * x + mix, ln_mix_g[i], ln_mix_b[i])
        x = _layernorm(DEEPNORM_ALPHA * x + _swiglu(x, ffn_w_gate[i], ffn_w_up[i], ffn_w_down[i]),
                       ln_ffn_g[i], ln_ffn_b[i])
    return x
```

```python
import functools
import math

import jax
import jax.numpy as jnp
from jax import lax
from jax.experimental import pallas as pl
from jax.experimental.pallas import tpu as pltpu

D_MODEL = 1024
DEPTH = 2
CHUNK = 64
N_HEADS = 8
HEAD_DIM_QK = 64
HEAD_DIM_V = 128
GMLP_BLOCK = 128
GMLP_HALF = 2 * D_MODEL
GMLP_GROUPS = 8
GMLP_GROUP_DIM = GMLP_HALF // GMLP_GROUPS
D_FF = 2816
DEEPNORM_ALPHA = (2 * DEPTH) ** 0.25
LN_EPS = 1e-5
MASK_VALUE = -1e30
QK_SCALE = HEAD_DIM_QK ** -0.5

V7X_VMEM_LIMIT_BYTES = 56 * 1024 * 1024

PROJ_TM = 1024
FFN_TM = 512
GMLP_TM = 256
ATTN_T = 256

_BF16 = jnp.bfloat16
_F32 = jnp.float32


def _lambda_init(layer_idx):
    return 0.8 - 0.6 * math.exp(-0.3 * layer_idx)


def _layernorm(y, g, b):
    mu = jnp.mean(y, axis=-1, keepdims=True)
    d = y - mu
    var = jnp.mean(d * d, axis=-1, keepdims=True)
    return d * lax.rsqrt(var + LN_EPS) * g + b


def _dot(a, b):
    return jnp.dot(a, b, preferred_element_type=_F32)


def _resident(shape):
    return pl.BlockSpec(shape, lambda *_: (0,) * len(shape), pipeline_mode=pl.Buffered(1))


def _params(n_parallel_axes=1):
    return pltpu.CompilerParams(
        dimension_semantics=("arbitrary",) * n_parallel_axes,
        vmem_limit_bytes=V7X_VMEM_LIMIT_BYTES,
    )


def _qkv_kernel(x_ref, w_ref, q_ref, k_ref, v_ref):
    xb = x_ref[...].astype(_BF16)
    for n, o_ref in enumerate((q_ref, k_ref, v_ref)):
        o_ref[...] = _dot(xb, w_ref[:, n * D_MODEL:(n + 1) * D_MODEL]).astype(_BF16)


def _qkv_proj(x2d, w_qkv):
    m = x2d.shape[0]
    out = jax.ShapeDtypeStruct((m, D_MODEL), _BF16)
    row_spec = pl.BlockSpec((PROJ_TM, D_MODEL), lambda i: (i, 0))
    return pl.pallas_call(
        _qkv_kernel,
        grid=(m // PROJ_TM,),
        in_specs=[row_spec, _resident((D_MODEL, 3 * D_MODEL))],
        out_specs=[row_spec, row_spec, row_spec],
        out_shape=[out, out, out],
        compiler_params=_params(),
        name="qkv_proj",
    )(x2d, w_qkv)


def _attn_kernel(slopes_ref, lam_ref, q_ref, k_ref, v_ref, g_ref, o_ref,
                 qs_ref, acc_ref, *, lambda_init):
    T = ATTN_T
    seq = q_ref.shape[1]
    h = pl.program_id(1)
    slope = slopes_ref[h]

    lp = lam_ref[...]
    lam = (jnp.exp(jnp.sum(lp[0:1] * lp[1:2], axis=-1, keepdims=True))
           - jnp.exp(jnp.sum(lp[2:3] * lp[3:4], axis=-1, keepdims=True))
           + lambda_init)

    row = lax.broadcasted_iota(jnp.int32, (T, T), 0)
    col = lax.broadcasted_iota(jnp.int32, (T, T), 1)
    ahead = jnp.maximum(col - row, 0).astype(_F32)
    allowed = (col // CHUNK) <= (row // CHUNK)
    diag_add = jnp.where(allowed, (-2.0 * slope) * ahead, MASK_VALUE)
    diag_add = jnp.concatenate([diag_add, diag_add], axis=0)

    key_iota = lax.broadcasted_iota(jnp.int32, (1, T), 1).astype(_F32)
    lane = lax.broadcasted_iota(jnp.int32, (T, 2 * HEAD_DIM_QK), 1)
    g_scaled = g_ref[...] * (1.0 - lambda_init)

    def scores(kj):
        kblk = k_ref[0, pl.ds(pl.multiple_of(kj * T, T), T), :]
        s = lax.dot_general(qs_ref[...], kblk, (((1,), (1,)), ((), ())),
                            preferred_element_type=_F32)
        key_bias = slope * (key_iota + (kj * T).astype(_F32))
        return s + key_bias

    def q_tile(qi, carry):
        q0 = pl.multiple_of(qi * T, T)
        q = q_ref[0, pl.ds(q0, T), :] * jnp.asarray(QK_SCALE, _BF16)
        zero = jnp.zeros_like(q)
        qs_ref[0:T, :] = jnp.where(lane < HEAD_DIM_QK, q, zero)
        qs_ref[T:2 * T, :] = jnp.where(lane >= HEAD_DIM_QK, q, zero)

        s = scores(qi) + diag_add
        m = jnp.max(s, axis=1, keepdims=True)
        p = jnp.exp(s - m)
        l = jnp.sum(p, axis=1, keepdims=True)
        acc_ref[...] = _dot(p.astype(_BF16), v_ref[0, pl.ds(q0, T), :])

        def kv_tile(kj, ml):
            m_prev, l_prev = ml
            s = scores(kj)
            m_new = jnp.maximum(m_prev, jnp.max(s, axis=1, keepdims=True))
            alpha = jnp.exp(m_prev - m_new)
            p = jnp.exp(s - m_new)
            l_new = alpha * l_prev + jnp.sum(p, axis=1, keepdims=True)
            vblk = v_ref[0, pl.ds(pl.multiple_of(kj * T, T), T), :]
            acc_ref[...] = alpha * acc_ref[...] + _dot(p.astype(_BF16), vblk)
            return m_new, l_new

        m, l = lax.fori_loop(0, qi, kv_tile, (m, l))

        acc = acc_ref[...]
        o = acc[0:T] / l[0:T] - lam * (acc[T:2 * T] / l[T:2 * T])
        ms = jnp.mean(o * o, axis=-1, keepdims=True)
        o = o * lax.rsqrt(ms + LN_EPS) * g_scaled
        o_ref[0, pl.ds(q0, T), :] = o.astype(_BF16)
        return carry

    lax.fori_loop(0, seq // T, q_tile, 0)


def _attention(q, k, v, slopes, lam_params, g_sub, lambda_init):
    b, s, _ = q.shape
    head_spec = pl.BlockSpec((1, s, HEAD_DIM_V), lambda bi, hi: (bi, 0, hi))
    return pl.pallas_call(
        functools.partial(_attn_kernel, lambda_init=lambda_init),
        grid=(b, N_HEADS),
        in_specs=[
            pl.BlockSpec(memory_space=pltpu.SMEM),
            pl.BlockSpec((4, HEAD_DIM_QK), lambda bi, hi: (0, 0)),
            head_spec, head_spec, head_spec,
            pl.BlockSpec((1, HEAD_DIM_V), lambda bi, hi: (0, 0)),
        ],
        out_specs=head_spec,
        out_shape=jax.ShapeDtypeStruct((b, s, N_HEADS * HEAD_DIM_V), _BF16),
        scratch_shapes=[
            pltpu.VMEM((2 * ATTN_T, 2 * HEAD_DIM_QK), _BF16),
            pltpu.VMEM((2 * ATTN_T, HEAD_DIM_V), _F32),
        ],
        compiler_params=_params(2),
        name="diff_attention",
    )(slopes, lam_params, q, k, v, g_sub)


def _out_proj_kernel(o_ref, w_ref, x_ref, g_ref, b_ref, y_ref):
    mix = _dot(o_ref[...], w_ref[...])
    y_ref[...] = _layernorm(DEEPNORM_ALPHA * x_ref[...] + mix, g_ref[...], b_ref[...])


def _out_proj(o2d, w_o, x2d, g, b):
    m = x2d.shape[0]
    row_spec = pl.BlockSpec((PROJ_TM, D_MODEL), lambda i: (i, 0))
    vec_spec = pl.BlockSpec((1, D_MODEL), lambda i: (0, 0))
    return pl.pallas_call(
        _out_proj_kernel,
        grid=(m // PROJ_TM,),
        in_specs=[row_spec, _resident((D_MODEL, D_MODEL)), row_spec, vec_spec, vec_spec],
        out_specs=row_spec,
        out_shape=jax.ShapeDtypeStruct((m, D_MODEL), _F32),
        compiler_params=_params(),
        name="out_proj_ln",
    )(o2d, w_o, x2d, g, b)


def _ffn_kernel(x_ref, wg_ref, wu_ref, wd_ref, g_ref, b_ref, y_ref):
    x = x_ref[...]
    xb = x.astype(_BF16)
    gate = _dot(xb, wg_ref[...])
    up = _dot(xb, wu_ref[...])
    act = (gate * jax.nn.sigmoid(gate) * up).astype(_BF16)
    y = _dot(act, wd_ref[...])
    y_ref[...] = _layernorm(DEEPNORM_ALPHA * x + y, g_ref[...], b_ref[...])


def _ffn(x2d, w_gate, w_up, w_down, g, b):
    m = x2d.shape[0]
    row_spec = pl.BlockSpec((FFN_TM, D_MODEL), lambda i: (i, 0))
    vec_spec = pl.BlockSpec((1, D_MODEL), lambda i: (0, 0))
    return pl.pallas_call(
        _ffn_kernel,
        grid=(m // FFN_TM,),
        in_specs=[row_spec,
                  _resident((D_MODEL, D_FF)), _resident((D_MODEL, D_FF)),
                  _resident((D_FF, D_MODEL)), vec_spec, vec_spec],
        out_specs=row_spec,
        out_shape=jax.ShapeDtypeStruct((m, D_MODEL), _F32),
        compiler_params=_params(),
        name="ffn_ln",
    )(x2d, w_gate, w_up, w_down, g, b)


def _gmlp_kernel(x_ref, win_ref, bin_ref, lng_ref, lnb_ref, ws_ref, bs_ref, wout_ref,
                 g_ref, b_ref, y_ref, gated_ref):
    x = x_ref[...]
    z = _dot(x.astype(_BF16), win_ref[...]) + bin_ref[...]
    z = 0.5 * z * (1.0 + lax.erf(z * (2.0 ** -0.5)))
    u = z[:, :GMLP_HALF]
    v = _layernorm(z[:, GMLP_HALF:], lng_ref[...], lnb_ref[...]).astype(_BF16)

    row = lax.broadcasted_iota(jnp.int32, (GMLP_BLOCK, GMLP_BLOCK), 0)
    col = lax.broadcasted_iota(jnp.int32, (GMLP_BLOCK, GMLP_BLOCK), 1)
    tri = (row >= col).astype(_F32)
    bs = bs_ref[...]
    for grp in range(GMLP_GROUPS):
        w_mix = (ws_ref[grp] * tri).astype(_BF16)
        cols = slice(grp * GMLP_GROUP_DIM, (grp + 1) * GMLP_GROUP_DIM)
        for blk in range(GMLP_TM // GMLP_BLOCK):
            rows = slice(blk * GMLP_BLOCK, (blk + 1) * GMLP_BLOCK)
            mixed = _dot(w_mix, v[rows, cols]) + bs[:, grp:grp + 1]
            gated_ref[rows, cols] = (u[rows, cols] * mixed).astype(_BF16)

    y = _dot(gated_ref[...], wout_ref[...])
    y_ref[...] = _layernorm(DEEPNORM_ALPHA * x + y, g_ref[...], b_ref[...])


def _gmlp(x2d, w_in, b_in, ln_g, ln_b, w_s, b_s_t, w_out, g, b):
    m = x2d.shape[0]
    row_spec = pl.BlockSpec((GMLP_TM, D_MODEL), lambda i: (i, 0))
    vec_spec = pl.BlockSpec((1, D_MODEL), lambda i: (0, 0))
    half_spec = pl.BlockSpec((1, GMLP_HALF), lambda i: (0, 0))
    return pl.pallas_call(
        _gmlp_kernel,
        grid=(m // GMLP_TM,),
        in_specs=[row_spec,
                  _resident((D_MODEL, 2 * GMLP_HALF)),
                  pl.BlockSpec((1, 2 * GMLP_HALF), lambda i: (0, 0)),
                  half_spec, half_spec,
                  _resident((GMLP_GROUPS, GMLP_BLOCK, GMLP_BLOCK)),
                  pl.BlockSpec((GMLP_BLOCK, GMLP_GROUPS), lambda i: (0, 0)),
                  _resident((GMLP_HALF, D_MODEL)),
                  vec_spec, vec_spec],
        out_specs=row_spec,
        out_shape=jax.ShapeDtypeStruct((m, D_MODEL), _F32),
        scratch_shapes=[pltpu.VMEM((GMLP_TM, GMLP_HALF), _BF16)],
        compiler_params=_params(),
        name="gmlp_ln",
    )(x2d, w_in, b_in, ln_g, ln_b, w_s, b_s_t, w_out, g, b)


def kernel(x, attn_w_qkv, attn_lambda_q1, attn_lambda_k1, attn_lambda_q2, attn_lambda_k2,
           attn_subln_g, attn_w_o, gmlp_w_in, gmlp_b_in, gmlp_ln_g, gmlp_ln_b, gmlp_w_s,
           gmlp_b_s, gmlp_w_out, ln_mix_g, ln_mix_b, ffn_w_gate, ffn_w_up, ffn_w_down,
           ln_ffn_g, ln_ffn_b):
    bsz, seq, d = x.shape
    assert d == D_MODEL and seq % ATTN_T == 0 and (bsz * seq) % PROJ_TM == 0
    m = bsz * seq
    x2d = x.reshape(m, d)
    slopes = 2.0 ** (-8.0 * jnp.arange(1, N_HEADS + 1, dtype=_F32) / N_HEADS)

    def vec(a):
        return a.reshape(1, -1)

    q, k, v = _qkv_proj(x2d, attn_w_qkv[0].astype(_BF16))
    lam_params = jnp.stack([attn_lambda_q1[0], attn_lambda_k1[0],
                            attn_lambda_q2[0], attn_lambda_k2[0]])
    o = _attention(q.reshape(bsz, seq, d), k.reshape(bsz, seq, d), v.reshape(bsz, seq, d),
                   slopes, lam_params, vec(attn_subln_g[0]), _lambda_init(0))
    x2d = _out_proj(o.reshape(m, d), attn_w_o[0].astype(_BF16), x2d,
                    vec(ln_mix_g[0]), vec(ln_mix_b[0]))
    x2d = _ffn(x2d, ffn_w_gate[0].astype(_BF16), ffn_w_up[0].astype(_BF16),
               ffn_w_down[0].astype(_BF16), vec(ln_ffn_g[0]), vec(ln_ffn_b[0]))

    x2d = _gmlp(x2d, gmlp_w_in[0].astype(_BF16), vec(gmlp_b_in[0]), vec(gmlp_ln_g[0]),
                vec(gmlp_ln_b[0]), gmlp_w_s[0], gmlp_b_s[0].T, gmlp_w_out[0].astype(_BF16),
                vec(ln_mix_g[1]), vec(ln_mix_b[1]))
    x2d = _ffn(x2d, ffn_w_gate[1].astype(_BF16), ffn_w_up[1].astype(_BF16),
               ffn_w_down[1].astype(_BF16), vec(ln_ffn_g[1]), vec(ln_ffn_b[1]))
    return x2d.reshape(bsz, seq, d)
```

```python
import functools
import math

import jax
import jax.numpy as jnp
from jax import lax
from jax.experimental import pallas as pl
from jax.experimental.pallas import tpu as pltpu

D_MODEL = 1024
DEPTH = 2
CHUNK = 64
N_HEADS = 8
HEAD_DIM_QK = 64
HEAD_DIM_V = 128
GMLP_BLOCK = 128
GMLP_HALF = 2 * D_MODEL
GMLP_GROUPS = 8
GMLP_GROUP_DIM = GMLP_HALF // GMLP_GROUPS
D_FF = 2816
DEEPNORM_ALPHA = (2 * DEPTH) ** 0.25
LN_EPS = 1e-5
MASK_VALUE = -1e30
QK_SCALE = HEAD_DIM_QK ** -0.5

V7X_VMEM_LIMIT_BYTES = 56 * 1024 * 1024

PROJ_TM = 1024
FFN_TM = 512
GMLP_TM = 256
ATTN_T = 256

_BF16 = jnp.bfloat16
_F32 = jnp.float32


def _lambda_init(layer_idx):
    return 0.8 - 0.6 * math.exp(-0.3 * layer_idx)


def _layernorm(y, g, b):
    mu = jnp.mean(y, axis=-1, keepdims=True)
    d = y - mu
    var = jnp.mean(d * d, axis=-1, keepdims=True)
    return d * lax.rsqrt(var + LN_EPS) * g + b


def _dot(a, b):
    return jnp.dot(a, b, preferred_element_type=_F32)


def _resident(shape):
    return pl.BlockSpec(shape, lambda *_: (0,) * len(shape), pipeline_mode=pl.Buffered(1))


def _params(n_parallel_axes=1):
    return pltpu.CompilerParams(
        dimension_semantics=("arbitrary",) * n_parallel_axes,
        vmem_limit_bytes=V7X_VMEM_LIMIT_BYTES,
    )


def _qkv_kernel(x_ref, w_ref, q_ref, k_ref, v_ref):
    xb = x_ref[...].astype(_BF16)
    for n, o_ref in enumerate((q_ref, k_ref, v_ref)):
        o_ref[...] = _dot(xb, w_ref[:, n * D_MODEL:(n + 1) * D_MODEL]).astype(_BF16)


def _qkv_proj(x2d, w_qkv):
    m = x2d.shape[0]
    out = jax.ShapeDtypeStruct((m, D_MODEL), _BF16)
    row_spec = pl.BlockSpec((PROJ_TM, D_MODEL), lambda i: (i, 0))
    return pl.pallas_call(
        _qkv_kernel,
        grid=(m // PROJ_TM,),
        in_specs=[row_spec, _resident((D_MODEL, 3 * D_MODEL))],
        out_specs=[row_spec, row_spec, row_spec],
        out_shape=[out, out, out],
        compiler_params=_params(),
        name="qkv_proj",
    )(x2d, w_qkv)


def _attn_kernel(slopes_ref, lam_ref, q_ref, k_ref, v_ref, g_ref, o_ref, *, lambda_init):
    T = ATTN_T
    seq = q_ref.shape[1]
    h = pl.program_id(1)
    slope = slopes_ref[h]

    lp = lam_ref[...]
    lam = (jnp.exp(jnp.sum(lp[0:1] * lp[1:2], axis=-1, keepdims=True))
           - jnp.exp(jnp.sum(lp[2:3] * lp[3:4], axis=-1, keepdims=True))
           + lambda_init)

    row = lax.broadcasted_iota(jnp.int32, (T, T), 0)
    col = lax.broadcasted_iota(jnp.int32, (T, T), 1)
    ahead = jnp.maximum(col - row, 0).astype(_F32)
    allowed = (col // CHUNK) <= (row // CHUNK)
    diag_add = jnp.where(allowed, (-2.0 * slope) * ahead, MASK_VALUE)
    diag_add = jnp.concatenate([diag_add, diag_add], axis=0)

    key_bias = slope * lax.broadcasted_iota(jnp.int32, (1, seq), 1).astype(_F32)
    lane = lax.broadcasted_iota(jnp.int32, (T, 2 * HEAD_DIM_QK), 1)
    g_scaled = g_ref[...] * (1.0 - lambda_init)

    def scores(qs, lo, hi):
        s = lax.dot_general(qs, k_ref[0, lo:hi, :], (((1,), (1,)), ((), ())),
                            preferred_element_type=_F32)
        return s + key_bias[:, lo:hi]

    for q0 in range(0, seq, T):
        q = q_ref[0, q0:q0 + T, :] * jnp.asarray(QK_SCALE, _BF16)
        zero = jnp.zeros_like(q)
        qs = jnp.concatenate([jnp.where(lane < HEAD_DIM_QK, q, zero),
                              jnp.where(lane >= HEAD_DIM_QK, q, zero)], axis=0)

        s_diag = scores(qs, q0, q0 + T) + diag_add
        m = jnp.max(s_diag, axis=1, keepdims=True)
        if q0 > 0:
            s_past = scores(qs, 0, q0)
            m = jnp.maximum(m, jnp.max(s_past, axis=1, keepdims=True))
        p = jnp.exp(s_diag - m)
        l = jnp.sum(p, axis=1, keepdims=True)
        acc = _dot(p.astype(_BF16), v_ref[0, q0:q0 + T, :])
        if q0 > 0:
            p = jnp.exp(s_past - m)
            l = l + jnp.sum(p, axis=1, keepdims=True)
            acc = acc + _dot(p.astype(_BF16), v_ref[0, 0:q0, :])

        o = acc[0:T] / l[0:T] - lam * (acc[T:2 * T] / l[T:2 * T])
        ms = jnp.mean(o * o, axis=-1, keepdims=True)
        o = o * lax.rsqrt(ms + LN_EPS) * g_scaled
        o_ref[0, q0:q0 + T, :] = o.astype(_BF16)


def _attention(q, k, v, slopes, lam_params, g_sub, lambda_init):
    b, s, _ = q.shape
    head_spec = pl.BlockSpec((1, s, HEAD_DIM_V), lambda bi, hi: (bi, 0, hi))
    return pl.pallas_call(
        functools.partial(_attn_kernel, lambda_init=lambda_init),
        grid=(b, N_HEADS),
        in_specs=[
            pl.BlockSpec(memory_space=pltpu.SMEM),
            pl.BlockSpec((4, HEAD_DIM_QK), lambda bi, hi: (0, 0)),
            head_spec, head_spec, head_spec,
            pl.BlockSpec((1, HEAD_DIM_V), lambda bi, hi: (0, 0)),
        ],
        out_specs=head_spec,
        out_shape=jax.ShapeDtypeStruct((b, s, N_HEADS * HEAD_DIM_V), _BF16),
        compiler_params=_params(2),
        name="diff_attention",
    )(slopes, lam_params, q, k, v, g_sub)


def _out_proj_kernel(o_ref, w_ref, x_ref, g_ref, b_ref, y_ref):
    mix = _dot(o_ref[...], w_ref[...])
    y_ref[...] = _layernorm(DEEPNORM_ALPHA * x_ref[...] + mix, g_ref[...], b_ref[...])


def _out_proj(o2d, w_o, x2d, g, b):
    m = x2d.shape[0]
    row_spec = pl.BlockSpec((PROJ_TM, D_MODEL), lambda i: (i, 0))
    vec_spec = pl.BlockSpec((1, D_MODEL), lambda i: (0, 0))
    return pl.pallas_call(
        _out_proj_kernel,
        grid=(m // PROJ_TM,),
        in_specs=[row_spec, _resident((D_MODEL, D_MODEL)), row_spec, vec_spec, vec_spec],
        out_specs=row_spec,
        out_shape=jax.ShapeDtypeStruct((m, D_MODEL), _F32),
        compiler_params=_params(),
        name="out_proj_ln",
    )(o2d, w_o, x2d, g, b)


def _ffn_kernel(x_ref, wg_ref, wu_ref, wd_ref, g_ref, b_ref, y_ref):
    x = x_ref[...]
    xb = x.astype(_BF16)
    gate = _dot(xb, wg_ref[...])
    up = _dot(xb, wu_ref[...])
    act = (gate * jax.nn.sigmoid(gate) * up).astype(_BF16)
    y = _dot(act, wd_ref[...])
    y_ref[...] = _layernorm(DEEPNORM_ALPHA * x + y, g_ref[...], b_ref[...])


def _ffn(x2d, w_gate, w_up, w_down, g, b):
    m = x2d.shape[0]
    row_spec = pl.BlockSpec((FFN_TM, D_MODEL), lambda i: (i, 0))
    vec_spec = pl.BlockSpec((1, D_MODEL), lambda i: (0, 0))
    return pl.pallas_call(
        _ffn_kernel,
        grid=(m // FFN_TM,),
        in_specs=[row_spec,
                  _resident((D_MODEL, D_FF)), _resident((D_MODEL, D_FF)),
                  _resident((D_FF, D_MODEL)), vec_spec, vec_spec],
        out_specs=row_spec,
        out_shape=jax.ShapeDtypeStruct((m, D_MODEL), _F32),
        compiler_params=_params(),
        name="ffn_ln",
    )(x2d, w_gate, w_up, w_down, g, b)


def _gmlp_kernel(x_ref, win_ref, bin_ref, lng_ref, lnb_ref, ws_ref, bs_ref, wout_ref,
                 g_ref, b_ref, y_ref, gated_ref):
    x = x_ref[...]
    z = _dot(x.astype(_BF16), win_ref[...]) + bin_ref[...]
    z = 0.5 * z * (1.0 + lax.erf(z * (2.0 ** -0.5)))
    u = z[:, :GMLP_HALF]
    v = _layernorm(z[:, GMLP_HALF:], lng_ref[...], lnb_ref[...]).astype(_BF16)

    row = lax.broadcasted_iota(jnp.int32, (GMLP_BLOCK, GMLP_BLOCK), 0)
    col = lax.broadcasted_iota(jnp.int32, (GMLP_BLOCK, GMLP_BLOCK), 1)
    tri = (row >= col).astype(_F32)
    bs = bs_ref[...]
    for grp in range(GMLP_GROUPS):
        w_mix = (ws_ref[grp] * tri).astype(_BF16)
        cols = slice(grp * GMLP_GROUP_DIM, (grp + 1) * GMLP_GROUP_DIM)
        for blk in range(GMLP_TM // GMLP_BLOCK):
            rows = slice(blk * GMLP_BLOCK, (blk + 1) * GMLP_BLOCK)
            mixed = _dot(w_mix, v[rows, cols]) + bs[:, grp:grp + 1]
            gated_ref[rows, cols] = (u[rows, cols] * mixed).astype(_BF16)

    y = _dot(gated_ref[...], wout_ref[...])
    y_ref[...] = _layernorm(DEEPNORM_ALPHA * x + y, g_ref[...], b_ref[...])


def _gmlp(x2d, w_in, b_in, ln_g, ln_b, w_s, b_s_t, w_out, g, b):
    m = x2d.shape[0]
    row_spec = pl.BlockSpec((GMLP_TM, D_MODEL), lambda i: (i, 0))
    vec_spec = pl.BlockSpec((1, D_MODEL), lambda i: (0, 0))
    half_spec = pl.BlockSpec((1, GMLP_HALF), lambda i: (0, 0))
    return pl.pallas_call(
        _gmlp_kernel,
        grid=(m // GMLP_TM,),
        in_specs=[row_spec,
                  _resident((D_MODEL, 2 * GMLP_HALF)),
                  pl.BlockSpec((1, 2 * GMLP_HALF), lambda i: (0, 0)),
                  half_spec, half_spec,
                  _resident((GMLP_GROUPS, GMLP_BLOCK, GMLP_BLOCK)),
                  pl.BlockSpec((GMLP_BLOCK, GMLP_GROUPS), lambda i: (0, 0)),
                  _resident((GMLP_HALF, D_MODEL)),
                  vec_spec, vec_spec],
        out_specs=row_spec,
        out_shape=jax.ShapeDtypeStruct((m, D_MODEL), _F32),
        scratch_shapes=[pltpu.VMEM((GMLP_TM, GMLP_HALF), _BF16)],
        compiler_params=_params(),
        name="gmlp_ln",
    )(x2d, w_in, b_in, ln_g, ln_b, w_s, b_s_t, w_out, g, b)


def kernel(x, attn_w_qkv, attn_lambda_q1, attn_lambda_k1, attn_lambda_q2, attn_lambda_k2,
           attn_subln_g, attn_w_o, gmlp_w_in, gmlp_b_in, gmlp_ln_g, gmlp_ln_b, gmlp_w_s,
           gmlp_b_s, gmlp_w_out, ln_mix_g, ln_mix_b, ffn_w_gate, ffn_w_up, ffn_w_down,
           ln_ffn_g, ln_ffn_b):
    bsz, seq, d = x.shape
    assert d == D_MODEL and seq % ATTN_T == 0 and (bsz * seq) % PROJ_TM == 0
    m = bsz * seq
    x2d = x.reshape(m, d)
    slopes = 2.0 ** (-8.0 * jnp.arange(1, N_HEADS + 1, dtype=_F32) / N_HEADS)

    def vec(a):
        return a.reshape(1, -1)

    q, k, v = _qkv_proj(x2d, attn_w_qkv[0].astype(_BF16))
    lam_params = jnp.stack([attn_lambda_q1[0], attn_lambda_k1[0],
                            attn_lambda_q2[0], attn_lambda_k2[0]])
    o = _attention(q.reshape(bsz, seq, d), k.reshape(bsz, seq, d), v.reshape(bsz, seq, d),
                   slopes, lam_params, vec(attn_subln_g[0]), _lambda_init(0))
    x2d = _out_proj(o.reshape(m, d), attn_w_o[0].astype(_BF16), x2d,
                    vec(ln_mix_g[0]), vec(ln_mix_b[0]))
    x2d = _ffn(x2d, ffn_w_gate[0].astype(_BF16), ffn_w_up[0].astype(_BF16),
               ffn_w_down[0].astype(_BF16), vec(ln_ffn_g[0]), vec(ln_ffn_b[0]))

    x2d = _gmlp(x2d, gmlp_w_in[0].astype(_BF16), vec(gmlp_b_in[0]), vec(gmlp_ln_g[0]),
                vec(gmlp_ln_b[0]), gmlp_w_s[0], gmlp_b_s[0].T, gmlp_w_out[0].astype(_BF16),
                vec(ln_mix_g[1]), vec(ln_mix_b[1]))
    x2d = _ffn(x2d, ffn_w_gate[1].astype(_BF16), ffn_w_up[1].astype(_BF16),
               ffn_w_down[1].astype(_BF16), vec(ln_ffn_g[1]), vec(ln_ffn_b[1]))
    return x2d.reshape(bsz, seq, d)
```

```python
import functools
import math

import jax
import jax.numpy as jnp
from jax import lax
from jax.experimental import pallas as pl
from jax.experimental.pallas import tpu as pltpu

D_MODEL = 1024
DEPTH = 2
CHUNK = 64
N_HEADS = 8
HEAD_DIM_QK = 64
HEAD_DIM_V = 128
GMLP_BLOCK = 128
GMLP_HALF = 2 * D_MODEL
GMLP_GROUPS = 8
GMLP_GROUP_DIM = GMLP_HALF // GMLP_GROUPS
D_FF = 2816
DEEPNORM_ALPHA = (2 * DEPTH) ** 0.25
LN_EPS = 1e-5
MASK_VALUE = -1e30
QK_SCALE = HEAD_DIM_QK ** -0.5
LOG2E = math.log2(math.e)

V7X_VMEM_LIMIT_BYTES = 56 * 1024 * 1024

PROJ_TM = 1024
FFN_TM = 512
GMLP_TM = 512
ATTN_T = 256

_BF16 = jnp.bfloat16
_F32 = jnp.float32


def _lambda_init(layer_idx):
    return 0.8 - 0.6 * math.exp(-0.3 * layer_idx)


def _layernorm(y, g, b):
    mu = jnp.mean(y, axis=-1, keepdims=True)
    d = y - mu
    var = jnp.mean(d * d, axis=-1, keepdims=True)
    return d * lax.rsqrt(var + LN_EPS) * g + b


def _dot(a, b):
    return jnp.dot(a, b, preferred_element_type=_F32)


def _resident(shape):
    return pl.BlockSpec(shape, lambda *_: (0,) * len(shape), pipeline_mode=pl.Buffered(1))


def _params(n_parallel_axes=1):
    return pltpu.CompilerParams(
        dimension_semantics=("arbitrary",) * n_parallel_axes,
        vmem_limit_bytes=V7X_VMEM_LIMIT_BYTES,
    )


def _qkv_kernel(x_ref, w_ref, q_ref, k_ref, v_ref):
    xb = x_ref[...].astype(_BF16)
    for n, (o_ref, scale) in enumerate(((q_ref, QK_SCALE * LOG2E), (k_ref, None), (v_ref, None))):
        y = _dot(xb, w_ref[:, n * D_MODEL:(n + 1) * D_MODEL])
        o_ref[...] = (y if scale is None else y * scale).astype(_BF16)


def _qkv_proj(x2d, w_qkv):
    m = x2d.shape[0]
    out = jax.ShapeDtypeStruct((m, D_MODEL), _BF16)
    row_spec = pl.BlockSpec((PROJ_TM, D_MODEL), lambda i: (i, 0))
    return pl.pallas_call(
        _qkv_kernel,
        grid=(m // PROJ_TM,),
        in_specs=[row_spec, _resident((D_MODEL, 3 * D_MODEL))],
        out_specs=[row_spec, row_spec, row_spec],
        out_shape=[out, out, out],
        compiler_params=_params(),
        name="qkv_proj",
    )(x2d, w_qkv)


def _attn_kernel(slopes_ref, lam_ref, q_ref, k_ref, v_ref, g_ref, o_ref, vones_ref, *,
                 lambda_init):
    T = ATTN_T
    DV = HEAD_DIM_V
    seq = q_ref.shape[1]
    h = pl.program_id(1)
    slope = slopes_ref[h] * LOG2E

    vones_ref[:, 0:DV] = v_ref[0]
    vones_ref[:, DV:2 * DV] = jnp.ones((seq, DV), _BF16)

    lp = lam_ref[...]
    lam = (jnp.exp(jnp.sum(lp[0:1] * lp[1:2], axis=-1, keepdims=True))
           - jnp.exp(jnp.sum(lp[2:3] * lp[3:4], axis=-1, keepdims=True))
           + lambda_init)

    row = lax.broadcasted_iota(jnp.int32, (T, T), 0)
    col = lax.broadcasted_iota(jnp.int32, (T, T), 1)
    ahead = jnp.maximum(col - row, 0).astype(_F32)
    allowed = (col // CHUNK) <= (row // CHUNK)
    diag_add = jnp.where(allowed, (-2.0 * slope) * ahead, MASK_VALUE)
    diag_add = jnp.concatenate([diag_add, diag_add], axis=0)

    key_bias = slope * lax.broadcasted_iota(jnp.int32, (1, seq), 1).astype(_F32)
    lane = lax.broadcasted_iota(jnp.int32, (T, 2 * HEAD_DIM_QK), 1)
    g_scaled = g_ref[...] * (1.0 - lambda_init)

    def scores(qs, lo, hi):
        s = lax.dot_general(qs, k_ref[0, lo:hi, :], (((1,), (1,)), ((), ())),
                            preferred_element_type=_F32)
        return s + key_bias[:, lo:hi]

    for q0 in range(0, seq, T):
        q = q_ref[0, q0:q0 + T, :]
        zero = jnp.zeros_like(q)
        qs = jnp.concatenate([jnp.where(lane < HEAD_DIM_QK, q, zero),
                              jnp.where(lane >= HEAD_DIM_QK, q, zero)], axis=0)

        s_diag = scores(qs, q0, q0 + T) + diag_add
        m = jnp.max(s_diag, axis=1, keepdims=True)
        if q0 > 0:
            s_past = scores(qs, 0, q0)
            m = jnp.maximum(m, jnp.max(s_past, axis=1, keepdims=True))
        p = jnp.exp2(s_diag - m)
        acc = _dot(p.astype(_BF16), vones_ref[q0:q0 + T, :])
        if q0 > 0:
            p = jnp.exp2(s_past - m)
            acc = acc + _dot(p.astype(_BF16), vones_ref[0:q0, :])

        att = acc[:, 0:DV] / acc[:, DV:2 * DV]
        o = att[0:T] - lam * att[T:2 * T]
        ms = jnp.mean(o * o, axis=-1, keepdims=True)
        o = o * lax.rsqrt(ms + LN_EPS) * g_scaled
        o_ref[0, q0:q0 + T, :] = o.astype(_BF16)


def _attention(q, k, v, slopes, lam_params, g_sub, lambda_init):
    b, s, _ = q.shape
    head_spec = pl.BlockSpec((1, s, HEAD_DIM_V), lambda bi, hi: (bi, 0, hi))
    return pl.pallas_call(
        functools.partial(_attn_kernel, lambda_init=lambda_init),
        grid=(b, N_HEADS),
        in_specs=[
            pl.BlockSpec(memory_space=pltpu.SMEM),
            pl.BlockSpec((4, HEAD_DIM_QK), lambda bi, hi: (0, 0)),
            head_spec, head_spec, head_spec,
            pl.BlockSpec((1, HEAD_DIM_V), lambda bi, hi: (0, 0)),
        ],
        out_specs=head_spec,
        out_shape=jax.ShapeDtypeStruct((b, s, N_HEADS * HEAD_DIM_V), _BF16),
        scratch_shapes=[pltpu.VMEM((s, 2 * HEAD_DIM_V), _BF16)],
        compiler_params=_params(2),
        name="diff_attention",
    )(slopes, lam_params, q, k, v, g_sub)


def _out_proj_kernel(o_ref, w_ref, x_ref, g_ref, b_ref, y_ref):
    mix = _dot(o_ref[...], w_ref[...])
    y_ref[...] = _layernorm(DEEPNORM_ALPHA * x_ref[...] + mix, g_ref[...], b_ref[...])


def _out_proj(o2d, w_o, x2d, g, b):
    m = x2d.shape[0]
    row_spec = pl.BlockSpec((PROJ_TM, D_MODEL), lambda i: (i, 0))
    vec_spec = pl.BlockSpec((1, D_MODEL), lambda i: (0, 0))
    return pl.pallas_call(
        _out_proj_kernel,
        grid=(m // PROJ_TM,),
        in_specs=[row_spec, _resident((D_MODEL, D_MODEL)), row_spec, vec_spec, vec_spec],
        out_specs=row_spec,
        out_shape=jax.ShapeDtypeStruct((m, D_MODEL), _F32),
        compiler_params=_params(),
        name="out_proj_ln",
    )(o2d, w_o, x2d, g, b)


def _ffn_kernel(x_ref, wg_ref, wu_ref, wd_ref, g_ref, b_ref, y_ref):
    x = x_ref[...]
    xb = x.astype(_BF16)
    gate = _dot(xb, wg_ref[...])
    up = _dot(xb, wu_ref[...])
    act = (gate * jax.nn.sigmoid(gate) * up).astype(_BF16)
    y = _dot(act, wd_ref[...])
    y_ref[...] = _layernorm(DEEPNORM_ALPHA * x + y, g_ref[...], b_ref[...])


def _ffn(x2d, w_gate, w_up, w_down, g, b):
    m = x2d.shape[0]
    row_spec = pl.BlockSpec((FFN_TM, D_MODEL), lambda i: (i, 0))
    vec_spec = pl.BlockSpec((1, D_MODEL), lambda i: (0, 0))
    return pl.pallas_call(
        _ffn_kernel,
        grid=(m // FFN_TM,),
        in_specs=[row_spec,
                  _resident((D_MODEL, D_FF)), _resident((D_MODEL, D_FF)),
                  _resident((D_FF, D_MODEL)), vec_spec, vec_spec],
        out_specs=row_spec,
        out_shape=jax.ShapeDtypeStruct((m, D_MODEL), _F32),
        compiler_params=_params(),
        name="ffn_ln",
    )(x2d, w_gate, w_up, w_down, g, b)


def _gmlp_kernel(x_ref, win_ref, bin_ref, lng_ref, lnb_ref, ws_ref, bs_ref, wout_ref,
                 g_ref, b_ref, y_ref, gated_ref):
    x = x_ref[...]
    z = _dot(x.astype(_BF16), win_ref[...]) + bin_ref[...]
    z = 0.5 * z * (1.0 + lax.erf(z * (2.0 ** -0.5)))
    u = z[:, :GMLP_HALF]
    v = _layernorm(z[:, GMLP_HALF:], lng_ref[...], lnb_ref[...]).astype(_BF16)

    row = lax.broadcasted_iota(jnp.int32, (GMLP_BLOCK, GMLP_BLOCK), 0)
    col = lax.broadcasted_iota(jnp.int32, (GMLP_BLOCK, GMLP_BLOCK), 1)
    tri = (row >= col).astype(_F32)
    bs = bs_ref[...]
    for grp in range(GMLP_GROUPS):
        w_mix = (ws_ref[grp] * tri).astype(_BF16)
        cols = slice(grp * GMLP_GROUP_DIM, (grp + 1) * GMLP_GROUP_DIM)
        for blk in range(GMLP_TM // GMLP_BLOCK):
            rows = slice(blk * GMLP_BLOCK, (blk + 1) * GMLP_BLOCK)
            mixed = _dot(w_mix, v[rows, cols]) + bs[:, grp:grp + 1]
            gated_ref[rows, cols] = (u[rows, cols] * mixed).astype(_BF16)

    y = _dot(gated_ref[...], wout_ref[...])
    y_ref[...] = _layernorm(DEEPNORM_ALPHA * x + y, g_ref[...], b_ref[...])


def _gmlp(x2d, w_in, b_in, ln_g, ln_b, w_s, b_s_t, w_out, g, b):
    m = x2d.shape[0]
    row_spec = pl.BlockSpec((GMLP_TM, D_MODEL), lambda i: (i, 0))
    vec_spec = pl.BlockSpec((1, D_MODEL), lambda i: (0, 0))
    half_spec = pl.BlockSpec((1, GMLP_HALF), lambda i: (0, 0))
    return pl.pallas_call(
        _gmlp_kernel,
        grid=(m // GMLP_TM,),
        in_specs=[row_spec,
                  _resident((D_MODEL, 2 * GMLP_HALF)),
                  pl.BlockSpec((1, 2 * GMLP_HALF), lambda i: (0, 0)),
                  half_spec, half_spec,
                  _resident((GMLP_GROUPS, GMLP_BLOCK, GMLP_BLOCK)),
                  pl.BlockSpec((GMLP_BLOCK, GMLP_GROUPS), lambda i: (0, 0)),
                  _resident((GMLP_HALF, D_MODEL)),
                  vec_spec, vec_spec],
        out_specs=row_spec,
        out_shape=jax.ShapeDtypeStruct((m, D_MODEL), _F32),
        scratch_shapes=[pltpu.VMEM((GMLP_TM, GMLP_HALF), _BF16)],
        compiler_params=_params(),
        name="gmlp_ln",
    )(x2d, w_in, b_in, ln_g, ln_b, w_s, b_s_t, w_out, g, b)


def kernel(x, attn_w_qkv, attn_lambda_q1, attn_lambda_k1, attn_lambda_q2, attn_lambda_k2,
           attn_subln_g, attn_w_o, gmlp_w_in, gmlp_b_in, gmlp_ln_g, gmlp_ln_b, gmlp_w_s,
           gmlp_b_s, gmlp_w_out, ln_mix_g, ln_mix_b, ffn_w_gate, ffn_w_up, ffn_w_down,
           ln_ffn_g, ln_ffn_b):
    bsz, seq, d = x.shape
    assert d == D_MODEL and seq % ATTN_T == 0 and (bsz * seq) % PROJ_TM == 0
    m = bsz * seq
    x2d = x.reshape(m, d)
    slopes = 2.0 ** (-8.0 * jnp.arange(1, N_HEADS + 1, dtype=_F32) / N_HEADS)

    def vec(a):
        return a.reshape(1, -1)

    q, k, v = _qkv_proj(x2d, attn_w_qkv[0].astype(_BF16))
    lam_params = jnp.stack([attn_lambda_q1[0], attn_lambda_k1[0],
                            attn_lambda_q2[0], attn_lambda_k2[0]])
    o = _attention(q.reshape(bsz, seq, d), k.reshape(bsz, seq, d), v.reshape(bsz, seq, d),
                   slopes, lam_params, vec(attn_subln_g[0]), _lambda_init(0))
    x2d = _out_proj(o.reshape(m, d), attn_w_o[0].astype(_BF16), x2d,
                    vec(ln_mix_g[0]), vec(ln_mix_b[0]))
    x2d = _ffn(x2d, ffn_w_gate[0].astype(_BF16), ffn_w_up[0].astype(_BF16),
               ffn_w_down[0].astype(_BF16), vec(ln_ffn_g[0]), vec(ln_ffn_b[0]))

    x2d = _gmlp(x2d, gmlp_w_in[0].astype(_BF16), vec(gmlp_b_in[0]), vec(gmlp_ln_g[0]),
                vec(gmlp_ln_b[0]), gmlp_w_s[0], gmlp_b_s[0].T, gmlp_w_out[0].astype(_BF16),
                vec(ln_mix_g[1]), vec(ln_mix_b[1]))
    x2d = _ffn(x2d, ffn_w_gate[1].astype(_BF16), ffn_w_up[1].astype(_BF16),
               ffn_w_down[1].astype(_BF16), vec(ln_ffn_g[1]), vec(ln_ffn_b[1]))
    return x2d.reshape(bsz, seq, d)
```

```python
import functools
import math
import struct

import jax
import jax.numpy as jnp
from jax import lax
from jax.experimental import pallas as pl
from jax.experimental.pallas import tpu as pltpu

D_MODEL = 1024
DEPTH = 2
CHUNK = 64
N_HEADS = 8
HEAD_DIM_QK = 64
HEAD_DIM_V = 128
GMLP_BLOCK = 128
GMLP_HALF = 2 * D_MODEL
GMLP_GROUPS = 8
GMLP_GROUP_DIM = GMLP_HALF // GMLP_GROUPS
D_FF = 2816
DEEPNORM_ALPHA = (2 * DEPTH) ** 0.25
LN_EPS = 1e-5
MASK_VALUE = -1e30
QK_SCALE = HEAD_DIM_QK ** -0.5
LOG2E = math.log2(math.e)

V7X_VMEM_LIMIT_BYTES = 56 * 1024 * 1024
LANES = 128

PROJ_TM = 1024
FFN_TM = 512
GMLP_TM = 512
ATTN_T = 256

_BF16 = jnp.bfloat16
_F32 = jnp.float32


def _bf16_round(x):
    bits = struct.unpack("<I", struct.pack("<f", x))[0]
    bits = (bits + 0x7FFF + ((bits >> 16) & 1)) & 0xFFFF0000
    return struct.unpack("<f", struct.pack("<I", bits))[0]


def _bf16_pieces(x, n=3):
    pieces = []
    for _ in range(n):
        p = _bf16_round(x)
        pieces.append(p)
        x -= p
    return pieces


def _lambda_init(layer_idx):
    return 0.8 - 0.6 * math.exp(-0.3 * layer_idx)


def _layernorm(y, g, b):
    mu = jnp.mean(y, axis=-1, keepdims=True)
    d = y - mu
    var = jnp.mean(d * d, axis=-1, keepdims=True)
    return d * lax.rsqrt(var + LN_EPS) * g + b


def _dot(a, b):
    return jnp.dot(a, b, preferred_element_type=_F32)


def _dot_nt(a, b):
    return lax.dot_general(a, b, (((1,), (1,)), ((), ())), preferred_element_type=_F32)


def _resident(shape):
    return pl.BlockSpec(shape, lambda *_: (0,) * len(shape), pipeline_mode=pl.Buffered(1))


def _params(n_parallel_axes=1):
    return pltpu.CompilerParams(
        dimension_semantics=("arbitrary",) * n_parallel_axes,
        vmem_limit_bytes=V7X_VMEM_LIMIT_BYTES,
    )


def _qkv_kernel(x_ref, wqt_ref, wk_ref, wvt_ref, qt_ref, k_ref, vt_ref):
    xb = x_ref[...].astype(_BF16)
    qt_ref[0] = (_dot_nt(wqt_ref[...], xb) * (QK_SCALE * LOG2E)).astype(_BF16)
    k_ref[...] = _dot(xb, wk_ref[...]).astype(_BF16)
    vt_ref[0] = _dot_nt(wvt_ref[...], xb).astype(_BF16)


def _qkv_proj(x2d, w_q_t, w_k, w_v_t, bsz, seq):
    m = x2d.shape[0]
    tiles_per_seq = seq // PROJ_TM
    row_spec = pl.BlockSpec((PROJ_TM, D_MODEL), lambda i: (i, 0))
    t_spec = pl.BlockSpec((1, D_MODEL, PROJ_TM),
                          lambda i: (i // tiles_per_seq, 0, i % tiles_per_seq))
    w_spec = _resident((D_MODEL, D_MODEL))
    t_shape = jax.ShapeDtypeStruct((bsz, D_MODEL, seq), _BF16)
    return pl.pallas_call(
        _qkv_kernel,
        grid=(m // PROJ_TM,),
        in_specs=[row_spec, w_spec, w_spec, w_spec],
        out_specs=[t_spec, row_spec, t_spec],
        out_shape=[t_shape, jax.ShapeDtypeStruct((m, D_MODEL), _BF16), t_shape],
        compiler_params=_params(),
        name="qkv_proj",
    )(x2d, w_q_t, w_k, w_v_t)


_LOG2E_PIECES = _bf16_pieces(LOG2E)


def _attn_kernel(slopes_ref, lam_ref, qt_ref, k_ref, vt_ref, g_ref, o_ref,
                 kaug_ref, qs_ref, s_ref, *, lambda_init):
    T = ATTN_T
    DQ = 2 * HEAD_DIM_QK
    seq = k_ref.shape[1]
    n_tiles = seq // T
    h = pl.program_id(1)
    slope = slopes_ref[h]

    pos = lax.broadcasted_iota(jnp.int32, (seq, DQ), 0)
    lane_k = lax.broadcasted_iota(jnp.int32, (seq, DQ), 1)
    j_hi = (pos // CHUNK).astype(_F32) * slope
    j_lo = (pos % CHUNK).astype(_F32) * slope
    key_aug = jnp.where(lane_k < 3, j_hi, jnp.where(lane_k < 6, j_lo, 0.0))
    kaug_ref[:, 0:DQ] = k_ref[0]
    kaug_ref[:, DQ:2 * DQ] = key_aug.astype(_BF16)

    row_q = lax.broadcasted_iota(jnp.int32, (DQ, 2 * T), 0)
    query_aug = jnp.zeros((DQ, 2 * T), _F32)
    for idx, piece in enumerate(_LOG2E_PIECES):
        query_aug = jnp.where(row_q == idx, CHUNK * piece, query_aug)
        query_aug = jnp.where(row_q == idx + 3, piece, query_aug)
    query_aug = query_aug.astype(_BF16)

    lp = lam_ref[...]
    lam = (jnp.exp(jnp.sum(lp[0:1] * lp[1:2], axis=-1, keepdims=True))
           - jnp.exp(jnp.sum(lp[2:3] * lp[3:4], axis=-1, keepdims=True))
           + lambda_init)

    key_i = lax.broadcasted_iota(jnp.int32, (T, T), 0)
    qry_i = lax.broadcasted_iota(jnp.int32, (T, T), 1)
    ahead = jnp.maximum(key_i - qry_i, 0).astype(_F32)
    allowed = (key_i // CHUNK) <= (qry_i // CHUNK)
    diag_add = jnp.where(allowed, (-2.0 * LOG2E * slope) * ahead, MASK_VALUE)
    diag_add = jnp.concatenate([diag_add, diag_add], axis=1)

    row_d = lax.broadcasted_iota(jnp.int32, (DQ, T), 0)
    g_scaled = g_ref[...] * (1.0 - lambda_init)

    def stage_queries(t):
        qt = qt_ref[0, :, t * T:(t + 1) * T]
        zero = jnp.zeros_like(qt)
        top = jnp.concatenate([jnp.where(row_d < HEAD_DIM_QK, qt, zero),
                               jnp.where(row_d >= HEAD_DIM_QK, qt, zero)], axis=1)
        qs_ref[t % 2] = jnp.concatenate([top, query_aug], axis=0)

    def score_pass(t, j, m):
        rows = slice(j * T, (j + 1) * T)
        s = _dot(kaug_ref[rows, :], qs_ref[t % 2])
        if j == t:
            s = s + diag_add
        s_ref[t % 2, rows, :] = s
        m_blk = jnp.max(s, axis=0, keepdims=True)
        return m_blk if m is None else jnp.maximum(m, m_blk)

    def value_pass(t, j, m, l, acc):
        rows = slice(j * T, (j + 1) * T)
        p = jnp.exp2(s_ref[t % 2, rows, :] - m)
        l_blk = jnp.sum(p, axis=0, keepdims=True)
        acc_blk = _dot(vt_ref[0, :, rows], p.astype(_BF16))
        if l is None:
            return l_blk, acc_blk
        return l + l_blk, acc + acc_blk

    def finish(t, l, acc):
        att = acc / l
        o = (att[:, 0:T] - lam * att[:, T:2 * T]).T
        ms = jnp.mean(o * o, axis=-1, keepdims=True)
        o = o * lax.rsqrt(ms + LN_EPS) * g_scaled
        o_ref[0, t * T:(t + 1) * T, :] = o.astype(_BF16)

    m_prev = None
    for t in range(n_tiles + 1):
        m_cur, l, acc = None, None, None
        if t < n_tiles:
            stage_queries(t)
        for j in range(t + 1):
            if t < n_tiles:
                m_cur = score_pass(t, j, m_cur)
            if t > 0 and j < t:
                l, acc = value_pass(t - 1, j, m_prev, l, acc)
        if t > 0:
            finish(t - 1, l, acc)
        m_prev = m_cur


def _attention(qt, k, vt, slopes, lam_params, g_sub, lambda_init):
    b, s, _ = k.shape
    head_spec = pl.BlockSpec((1, s, HEAD_DIM_V), lambda bi, hi: (bi, 0, hi))
    head_t_spec = pl.BlockSpec((1, HEAD_DIM_V, s), lambda bi, hi: (bi, hi, 0))
    return pl.pallas_call(
        functools.partial(_attn_kernel, lambda_init=lambda_init),
        grid=(b, N_HEADS),
        in_specs=[
            pl.BlockSpec(memory_space=pltpu.SMEM),
            pl.BlockSpec((4, HEAD_DIM_QK), lambda bi, hi: (0, 0)),
            head_t_spec, head_spec, head_t_spec,
            pl.BlockSpec((1, HEAD_DIM_V), lambda bi, hi: (0, 0)),
        ],
        out_specs=head_spec,
        out_shape=jax.ShapeDtypeStruct((b, s, N_HEADS * HEAD_DIM_V), _BF16),
        scratch_shapes=[
            pltpu.VMEM((s, 4 * HEAD_DIM_QK), _BF16),
            pltpu.VMEM((2, 4 * HEAD_DIM_QK, 2 * ATTN_T), _BF16),
            pltpu.VMEM((2, s, 2 * ATTN_T), _F32),
        ],
        compiler_params=_params(2),
        name="diff_attention",
    )(slopes, lam_params, qt, k, vt, g_sub)


def _out_proj_kernel(o_ref, w_ref, x_ref, g_ref, b_ref, y_ref):
    mix = _dot(o_ref[...], w_ref[...])
    y_ref[...] = _layernorm(DEEPNORM_ALPHA * x_ref[...] + mix, g_ref[...], b_ref[...])


def _out_proj(o2d, w_o, x2d, g, b):
    m = x2d.shape[0]
    row_spec = pl.BlockSpec((PROJ_TM, D_MODEL), lambda i: (i, 0))
    vec_spec = pl.BlockSpec((1, D_MODEL), lambda i: (0, 0))
    return pl.pallas_call(
        _out_proj_kernel,
        grid=(m // PROJ_TM,),
        in_specs=[row_spec, _resident((D_MODEL, D_MODEL)), row_spec, vec_spec, vec_spec],
        out_specs=row_spec,
        out_shape=jax.ShapeDtypeStruct((m, D_MODEL), _F32),
        compiler_params=_params(),
        name="out_proj_ln",
    )(o2d, w_o, x2d, g, b)


def _ffn_kernel(x_ref, wg_ref, wu_ref, wd_ref, g_ref, b_ref, y_ref):
    x = x_ref[...]
    xb = x.astype(_BF16)
    gate = _dot(xb, wg_ref[...])
    up = _dot(xb, wu_ref[...])
    act = (gate * jax.nn.sigmoid(gate) * up).astype(_BF16)
    y = _dot(act, wd_ref[...])
    y_ref[...] = _layernorm(DEEPNORM_ALPHA * x + y, g_ref[...], b_ref[...])


def _ffn(x2d, w_gate, w_up, w_down, g, b):
    m = x2d.shape[0]
    row_spec = pl.BlockSpec((FFN_TM, D_MODEL), lambda i: (i, 0))
    vec_spec = pl.BlockSpec((1, D_MODEL), lambda i: (0, 0))
    return pl.pallas_call(
        _ffn_kernel,
        grid=(m // FFN_TM,),
        in_specs=[row_spec,
                  _resident((D_MODEL, D_FF)), _resident((D_MODEL, D_FF)),
                  _resident((D_FF, D_MODEL)), vec_spec, vec_spec],
        out_specs=row_spec,
        out_shape=jax.ShapeDtypeStruct((m, D_MODEL), _F32),
        compiler_params=_params(),
        name="ffn_ln",
    )(x2d, w_gate, w_up, w_down, g, b)


def _gmlp_kernel(x_ref, win_ref, bin_ref, lng_ref, lnb_ref, ws_ref, bs_ref, wout_ref,
                 g_ref, b_ref, y_ref, gated_ref):
    x = x_ref[...]
    z = _dot(x.astype(_BF16), win_ref[...]) + bin_ref[...]
    z = 0.5 * z * (1.0 + lax.erf(z * (2.0 ** -0.5)))
    u = z[:, :GMLP_HALF]
    v = _layernorm(z[:, GMLP_HALF:], lng_ref[...], lnb_ref[...]).astype(_BF16)

    row = lax.broadcasted_iota(jnp.int32, (GMLP_BLOCK, GMLP_BLOCK), 0)
    col = lax.broadcasted_iota(jnp.int32, (GMLP_BLOCK, GMLP_BLOCK), 1)
    tri = (row >= col).astype(_F32)
    bs = bs_ref[...]
    for grp in range(GMLP_GROUPS):
        w_mix = (ws_ref[grp] * tri).astype(_BF16)
        cols = slice(grp * GMLP_GROUP_DIM, (grp + 1) * GMLP_GROUP_DIM)
        for blk in range(GMLP_TM // GMLP_BLOCK):
            rows = slice(blk * GMLP_BLOCK, (blk + 1) * GMLP_BLOCK)
            mixed = _dot(w_mix, v[rows, cols]) + bs[:, grp:grp + 1]
            gated_ref[rows, cols] = (u[rows, cols] * mixed).astype(_BF16)

    y = _dot(gated_ref[...], wout_ref[...])
    y_ref[...] = _layernorm(DEEPNORM_ALPHA * x + y, g_ref[...], b_ref[...])


def _gmlp(x2d, w_in, b_in, ln_g, ln_b, w_s, b_s_t, w_out, g, b):
    m = x2d.shape[0]
    row_spec = pl.BlockSpec((GMLP_TM, D_MODEL), lambda i: (i, 0))
    vec_spec = pl.BlockSpec((1, D_MODEL), lambda i: (0, 0))
    half_spec = pl.BlockSpec((1, GMLP_HALF), lambda i: (0, 0))
    return pl.pallas_call(
        _gmlp_kernel,
        grid=(m // GMLP_TM,),
        in_specs=[row_spec,
                  _resident((D_MODEL, 2 * GMLP_HALF)),
                  pl.BlockSpec((1, 2 * GMLP_HALF), lambda i: (0, 0)),
                  half_spec, half_spec,
                  _resident((GMLP_GROUPS, GMLP_BLOCK, GMLP_BLOCK)),
                  pl.BlockSpec((GMLP_BLOCK, GMLP_GROUPS), lambda i: (0, 0)),
                  _resident((GMLP_HALF, D_MODEL)),
                  vec_spec, vec_spec],
        out_specs=row_spec,
        out_shape=jax.ShapeDtypeStruct((m, D_MODEL), _F32),
        scratch_shapes=[pltpu.VMEM((GMLP_TM, GMLP_HALF), _BF16)],
        compiler_params=_params(),
        name="gmlp_ln",
    )(x2d, w_in, b_in, ln_g, ln_b, w_s, b_s_t, w_out, g, b)


def kernel(x, attn_w_qkv, attn_lambda_q1, attn_lambda_k1, attn_lambda_q2, attn_lambda_k2,
           attn_subln_g, attn_w_o, gmlp_w_in, gmlp_b_in, gmlp_ln_g, gmlp_ln_b, gmlp_w_s,
           gmlp_b_s, gmlp_w_out, ln_mix_g, ln_mix_b, ffn_w_gate, ffn_w_up, ffn_w_down,
           ln_ffn_g, ln_ffn_b):
    bsz, seq, d = x.shape
    assert d == D_MODEL and seq % PROJ_TM == 0 and seq % ATTN_T == 0
    m = bsz * seq
    x2d = x.reshape(m, d)
    slopes = 2.0 ** (-8.0 * jnp.arange(1, N_HEADS + 1, dtype=_F32) / N_HEADS)

    def vec(a):
        return a.reshape(1, -1)

    w_qkv = attn_w_qkv[0]
    qt, k, vt = _qkv_proj(x2d, w_qkv[:, :D_MODEL].T.astype(_BF16),
                          w_qkv[:, D_MODEL:2 * D_MODEL].astype(_BF16),
                          w_qkv[:, 2 * D_MODEL:].T.astype(_BF16), bsz, seq)
    lam_params = jnp.stack([attn_lambda_q1[0], attn_lambda_k1[0],
                            attn_lambda_q2[0], attn_lambda_k2[0]])
    o = _attention(qt, k.reshape(bsz, seq, d), vt,
                   slopes, lam_params, vec(attn_subln_g[0]), _lambda_init(0))
    x2d = _out_proj(o.reshape(m, d), attn_w_o[0].astype(_BF16), x2d,
                    vec(ln_mix_g[0]), vec(ln_mix_b[0]))
    x2d = _ffn(x2d, ffn_w_gate[0].astype(_BF16), ffn_w_up[0].astype(_BF16),
               ffn_w_down[0].astype(_BF16), vec(ln_ffn_g[0]), vec(ln_ffn_b[0]))

    x2d = _gmlp(x2d, gmlp_w_in[0].astype(_BF16), vec(gmlp_b_in[0]), vec(gmlp_ln_g[0]),
                vec(gmlp_ln_b[0]), gmlp_w_s[0], gmlp_b_s[0].T, gmlp_w_out[0].astype(_BF16),
                vec(ln_mix_g[1]), vec(ln_mix_b[1]))
    x2d = _ffn(x2d, ffn_w_gate[1].astype(_BF16), ffn_w_up[1].astype(_BF16),
               ffn_w_down[1].astype(_BF16), vec(ln_ffn_g[1]), vec(ln_ffn_b[1]))
    return x2d.reshape(bsz, seq, d)
```

```python
import functools
import math
import struct

import jax
import jax.numpy as jnp
from jax import lax
from jax.experimental import pallas as pl
from jax.experimental.pallas import tpu as pltpu

D_MODEL = 1024
DEPTH = 2
CHUNK = 64
N_HEADS = 8
HEAD_DIM_QK = 64
HEAD_DIM_V = 128
GMLP_BLOCK = 128
GMLP_HALF = 2 * D_MODEL
GMLP_GROUPS = 8
GMLP_GROUP_DIM = GMLP_HALF // GMLP_GROUPS
D_FF = 2816
DEEPNORM_ALPHA = (2 * DEPTH) ** 0.25
LN_EPS = 1e-5
MASK_VALUE = -1e30
QK_SCALE = HEAD_DIM_QK ** -0.5
LOG2E = math.log2(math.e)

V7X_VMEM_LIMIT_BYTES = 56 * 1024 * 1024
BF16_SUBLANES = 16

PROJ_TM = 512
FFN_TM = 512
GMLP_TM = 512
ATTN_T = 256

_BF16 = jnp.bfloat16
_F32 = jnp.float32


def _bf16_round(x):
    bits = struct.unpack("<I", struct.pack("<f", x))[0]
    bits = (bits + 0x7FFF + ((bits >> 16) & 1)) & 0xFFFF0000
    return struct.unpack("<f", struct.pack("<I", bits))[0]


def _bf16_pieces(x, n=3):
    pieces = []
    for _ in range(n):
        p = _bf16_round(x)
        pieces.append(p)
        x -= p
    return pieces


def _lambda_init(layer_idx):
    return 0.8 - 0.6 * math.exp(-0.3 * layer_idx)


def _layernorm(y, g, b):
    mu = jnp.mean(y, axis=-1, keepdims=True)
    d = y - mu
    var = jnp.mean(d * d, axis=-1, keepdims=True)
    return d * lax.rsqrt(var + LN_EPS) * g + b


def _dot(a, b):
    return jnp.dot(a, b, preferred_element_type=_F32)


def _dot_nt(a, b):
    return lax.dot_general(a, b, (((1,), (1,)), ((), ())), preferred_element_type=_F32)


def _resident(shape):
    return pl.BlockSpec(shape, lambda *_: (0,) * len(shape), pipeline_mode=pl.Buffered(1))


def _params(n_parallel_axes=1):
    return pltpu.CompilerParams(
        dimension_semantics=("arbitrary",) * n_parallel_axes,
        vmem_limit_bytes=V7X_VMEM_LIMIT_BYTES,
    )


def _qkv_kernel(x_ref, wqt_ref, wk_ref, wvt_ref, *refs, n_cast):
    cast_src = refs[:n_cast]
    qt_ref, k_ref, vt_ref = refs[n_cast:n_cast + 3]
    cast_dst = refs[n_cast + 3:]
    xb = x_ref[...].astype(_BF16)
    qt_ref[0] = (_dot_nt(wqt_ref[...], xb) * (QK_SCALE * LOG2E)).astype(_BF16)
    k_ref[...] = _dot(xb, wk_ref[...]).astype(_BF16)
    vt_ref[0] = _dot_nt(wvt_ref[...], xb).astype(_BF16)
    for src, dst in zip(cast_src, cast_dst):
        dst[...] = src[...].astype(_BF16)


def _qkv_proj(x2d, w_q_t, w_k, w_v_t, bsz, seq, later_weights):
    m = x2d.shape[0]
    steps = m // PROJ_TM
    tiles_per_seq = seq // PROJ_TM
    row_spec = pl.BlockSpec((PROJ_TM, D_MODEL), lambda i: (i, 0))
    t_spec = pl.BlockSpec((1, D_MODEL, PROJ_TM),
                          lambda i: (i // tiles_per_seq, 0, i % tiles_per_seq))
    w_spec = _resident((D_MODEL, D_MODEL))
    t_shape = jax.ShapeDtypeStruct((bsz, D_MODEL, seq), _BF16)

    flat = [w.reshape(-1, w.shape[-1]) for w in later_weights]
    for w in flat:
        assert w.shape[0] % (steps * BF16_SUBLANES) == 0, w.shape
    cast_specs = [pl.BlockSpec((w.shape[0] // steps, w.shape[1]), lambda i: (i, 0)) for w in flat]
    cast_shapes = [jax.ShapeDtypeStruct(w.shape, _BF16) for w in flat]

    outs = pl.pallas_call(
        functools.partial(_qkv_kernel, n_cast=len(flat)),
        grid=(steps,),
        in_specs=[row_spec, w_spec, w_spec, w_spec] + cast_specs,
        out_specs=[t_spec, row_spec, t_spec] + cast_specs,
        out_shape=[t_shape, jax.ShapeDtypeStruct((m, D_MODEL), _BF16), t_shape] + cast_shapes,
        compiler_params=_params(),
        name="qkv_proj",
    )(x2d, w_q_t, w_k, w_v_t, *flat)
    cast = [c.reshape(w.shape) for c, w in zip(outs[3:], later_weights)]
    return outs[0], outs[1], outs[2], cast


_LOG2E_PIECES = _bf16_pieces(LOG2E)


def _attn_kernel(slopes_ref, lam_ref, qt_ref, k_ref, vt_ref, g_ref, o_ref,
                 kaug_ref, qs_ref, s_ref, *, lambda_init):
    T = ATTN_T
    DQ = 2 * HEAD_DIM_QK
    seq = k_ref.shape[1]
    n_tiles = seq // T
    h = pl.program_id(1)
    slope = slopes_ref[h]

    pos = lax.broadcasted_iota(jnp.int32, (seq, DQ), 0)
    lane_k = lax.broadcasted_iota(jnp.int32, (seq, DQ), 1)
    j_hi = (pos // CHUNK).astype(_F32) * slope
    j_lo = (pos % CHUNK).astype(_F32) * slope
    key_aug = jnp.where(lane_k < 3, j_hi, jnp.where(lane_k < 6, j_lo, 0.0))
    kaug_ref[:, 0:DQ] = k_ref[0]
    kaug_ref[:, DQ:2 * DQ] = key_aug.astype(_BF16)

    row_q = lax.broadcasted_iota(jnp.int32, (DQ, 2 * T), 0)
    query_aug = jnp.zeros((DQ, 2 * T), _F32)
    for idx, piece in enumerate(_LOG2E_PIECES):
        query_aug = jnp.where(row_q == idx, CHUNK * piece, query_aug)
        query_aug = jnp.where(row_q == idx + 3, piece, query_aug)
    query_aug = query_aug.astype(_BF16)

    lp = lam_ref[...]
    lam = (jnp.exp(jnp.sum(lp[0:1] * lp[1:2], axis=-1, keepdims=True))
           - jnp.exp(jnp.sum(lp[2:3] * lp[3:4], axis=-1, keepdims=True))
           + lambda_init)

    key_i = lax.broadcasted_iota(jnp.int32, (T, T), 0)
    qry_i = lax.broadcasted_iota(jnp.int32, (T, T), 1)
    ahead = jnp.maximum(key_i - qry_i, 0).astype(_F32)
    allowed = (key_i // CHUNK) <= (qry_i // CHUNK)
    diag_add = jnp.where(allowed, (-2.0 * LOG2E * slope) * ahead, MASK_VALUE)
    diag_add = jnp.concatenate([diag_add, diag_add], axis=1)

    row_d = lax.broadcasted_iota(jnp.int32, (DQ, T), 0)
    g_scaled = g_ref[...] * (1.0 - lambda_init)

    def stage_queries(t):
        qt = qt_ref[0, :, t * T:(t + 1) * T]
        zero = jnp.zeros_like(qt)
        top = jnp.concatenate([jnp.where(row_d < HEAD_DIM_QK, qt, zero),
                               jnp.where(row_d >= HEAD_DIM_QK, qt, zero)], axis=1)
        qs_ref[t % 2] = jnp.concatenate([top, query_aug], axis=0)

    def score_pass(t, j, m):
        rows = slice(j * T, (j + 1) * T)
        s = _dot(kaug_ref[rows, :], qs_ref[t % 2])
        if j == t:
            s = s + diag_add
        s_ref[t % 2, rows, :] = s
        m_blk = jnp.max(s, axis=0, keepdims=True)
        return m_blk if m is None else jnp.maximum(m, m_blk)

    def value_pass(t, j, m, l, acc):
        rows = slice(j * T, (j + 1) * T)
        p = jnp.exp2(s_ref[t % 2, rows, :] - m)
        l_blk = jnp.sum(p, axis=0, keepdims=True)
        acc_blk = _dot(vt_ref[0, :, rows], p.astype(_BF16))
        if l is None:
            return l_blk, acc_blk
        return l + l_blk, acc + acc_blk

    def finish(t, l, acc):
        att = acc / l
        o = (att[:, 0:T] - lam * att[:, T:2 * T]).T
        ms = jnp.mean(o * o, axis=-1, keepdims=True)
        o = o * lax.rsqrt(ms + LN_EPS) * g_scaled
        o_ref[0, t * T:(t + 1) * T, :] = o.astype(_BF16)

    m_prev = None
    for t in range(n_tiles + 1):
        m_cur, l, acc = None, None, None
        if t < n_tiles:
            stage_queries(t)
        for j in range(t + 1):
            if t < n_tiles:
                m_cur = score_pass(t, j, m_cur)
            if t > 0 and j < t:
                l, acc = value_pass(t - 1, j, m_prev, l, acc)
        if t > 0:
            finish(t - 1, l, acc)
        m_prev = m_cur


def _attention(qt, k, vt, slopes, lam_params, g_sub, lambda_init):
    b, s, _ = k.shape
    head_spec = pl.BlockSpec((1, s, HEAD_DIM_V), lambda bi, hi: (bi, 0, hi))
    head_t_spec = pl.BlockSpec((1, HEAD_DIM_V, s), lambda bi, hi: (bi, hi, 0))
    return pl.pallas_call(
        functools.partial(_attn_kernel, lambda_init=lambda_init),
        grid=(b, N_HEADS),
        in_specs=[
            pl.BlockSpec(memory_space=pltpu.SMEM),
            pl.BlockSpec((4, HEAD_DIM_QK), lambda bi, hi: (0, 0)),
            head_t_spec, head_spec, head_t_spec,
            pl.BlockSpec((1, HEAD_DIM_V), lambda bi, hi: (0, 0)),
        ],
        out_specs=head_spec,
        out_shape=jax.ShapeDtypeStruct((b, s, N_HEADS * HEAD_DIM_V), _BF16),
        scratch_shapes=[
            pltpu.VMEM((s, 4 * HEAD_DIM_QK), _BF16),
            pltpu.VMEM((2, 4 * HEAD_DIM_QK, 2 * ATTN_T), _BF16),
            pltpu.VMEM((2, s, 2 * ATTN_T), _F32),
        ],
        compiler_params=_params(2),
        name="diff_attention",
    )(slopes, lam_params, qt, k, vt, g_sub)


def _ffn_kernel(*refs, with_out_proj):
    if with_out_proj:
        o_ref, wo_ref, mix_g_ref, mix_b_ref = refs[:4]
        refs = refs[4:]
    x_ref, wg_ref, wu_ref, wd_ref, g_ref, b_ref, y_ref = refs
    x = x_ref[...]
    if with_out_proj:
        mix = _dot(o_ref[...], wo_ref[0])
        x = _layernorm(DEEPNORM_ALPHA * x + mix, mix_g_ref[...], mix_b_ref[...])
    xb = x.astype(_BF16)
    gate = _dot(xb, wg_ref[0])
    up = _dot(xb, wu_ref[0])
    act = (gate * jax.nn.sigmoid(gate) * up).astype(_BF16)
    y = _dot(act, wd_ref[0])
    y_ref[...] = _layernorm(DEEPNORM_ALPHA * x + y, g_ref[...], b_ref[...])


def _layer_weight(shape, layer):
    return pl.BlockSpec((1,) + shape, lambda i: (layer, 0, 0), pipeline_mode=pl.Buffered(1))


def _ffn(x2d, w_gate, w_up, w_down, layer, g, b, out_proj=None):
    m = x2d.shape[0]
    row_spec = pl.BlockSpec((FFN_TM, D_MODEL), lambda i: (i, 0))
    vec_spec = pl.BlockSpec((1, D_MODEL), lambda i: (0, 0))
    in_specs = [row_spec, _layer_weight((D_MODEL, D_FF), layer),
                _layer_weight((D_MODEL, D_FF), layer), _layer_weight((D_FF, D_MODEL), layer),
                vec_spec, vec_spec]
    args = [x2d, w_gate, w_up, w_down, g, b]
    if out_proj is not None:
        in_specs = [row_spec, _layer_weight((D_MODEL, D_MODEL), 0), vec_spec, vec_spec] + in_specs
        args = list(out_proj) + args
    return pl.pallas_call(
        functools.partial(_ffn_kernel, with_out_proj=out_proj is not None),
        grid=(m // FFN_TM,),
        in_specs=in_specs,
        out_specs=row_spec,
        out_shape=jax.ShapeDtypeStruct((m, D_MODEL), _F32),
        compiler_params=_params(),
        name="ffn_ln",
    )(*args)


def _gmlp_kernel(x_ref, win_ref, bin_ref, lng_ref, lnb_ref, ws_ref, bs_ref, wout_ref,
                 g_ref, b_ref, y_ref, gated_ref):
    x = x_ref[...]
    z = _dot(x.astype(_BF16), win_ref[...]) + bin_ref[...]
    z = 0.5 * z * (1.0 + lax.erf(z * (2.0 ** -0.5)))
    u = z[:, :GMLP_HALF]
    v = _layernorm(z[:, GMLP_HALF:], lng_ref[...], lnb_ref[...]).astype(_BF16)

    row = lax.broadcasted_iota(jnp.int32, (GMLP_BLOCK, GMLP_BLOCK), 0)
    col = lax.broadcasted_iota(jnp.int32, (GMLP_BLOCK, GMLP_BLOCK), 1)
    tri = (row >= col).astype(_F32)
    bs = bs_ref[...]
    for grp in range(GMLP_GROUPS):
        w_mix = (ws_ref[grp] * tri).astype(_BF16)
        cols = slice(grp * GMLP_GROUP_DIM, (grp + 1) * GMLP_GROUP_DIM)
        for blk in range(GMLP_TM // GMLP_BLOCK):
            rows = slice(blk * GMLP_BLOCK, (blk + 1) * GMLP_BLOCK)
            mixed = _dot(w_mix, v[rows, cols]) + bs[:, grp:grp + 1]
            gated_ref[rows, cols] = (u[rows, cols] * mixed).astype(_BF16)

    y = _dot(gated_ref[...], wout_ref[...])
    y_ref[...] = _layernorm(DEEPNORM_ALPHA * x + y, g_ref[...], b_ref[...])


def _gmlp(x2d, w_in, b_in, ln_g, ln_b, w_s, b_s_t, w_out, g, b):
    m = x2d.shape[0]
    row_spec = pl.BlockSpec((GMLP_TM, D_MODEL), lambda i: (i, 0))
    vec_spec = pl.BlockSpec((1, D_MODEL), lambda i: (0, 0))
    half_spec = pl.BlockSpec((1, GMLP_HALF), lambda i: (0, 0))
    return pl.pallas_call(
        _gmlp_kernel,
        grid=(m // GMLP_TM,),
        in_specs=[row_spec,
                  _resident((D_MODEL, 2 * GMLP_HALF)),
                  pl.BlockSpec((1, 2 * GMLP_HALF), lambda i: (0, 0)),
                  half_spec, half_spec,
                  _resident((GMLP_GROUPS, GMLP_BLOCK, GMLP_BLOCK)),
                  pl.BlockSpec((GMLP_BLOCK, GMLP_GROUPS), lambda i: (0, 0)),
                  _resident((GMLP_HALF, D_MODEL)),
                  vec_spec, vec_spec],
        out_specs=row_spec,
        out_shape=jax.ShapeDtypeStruct((m, D_MODEL), _F32),
        scratch_shapes=[pltpu.VMEM((GMLP_TM, GMLP_HALF), _BF16)],
        compiler_params=_params(),
        name="gmlp_ln",
    )(x2d, w_in, b_in, ln_g, ln_b, w_s, b_s_t, w_out, g, b)


def kernel(x, attn_w_qkv, attn_lambda_q1, attn_lambda_k1, attn_lambda_q2, attn_lambda_k2,
           attn_subln_g, attn_w_o, gmlp_w_in, gmlp_b_in, gmlp_ln_g, gmlp_ln_b, gmlp_w_s,
           gmlp_b_s, gmlp_w_out, ln_mix_g, ln_mix_b, ffn_w_gate, ffn_w_up, ffn_w_down,
           ln_ffn_g, ln_ffn_b):
    bsz, seq, d = x.shape
    assert d == D_MODEL and seq % PROJ_TM == 0 and seq % ATTN_T == 0
    m = bsz * seq
    x2d = x.reshape(m, d)
    slopes = 2.0 ** (-8.0 * jnp.arange(1, N_HEADS + 1, dtype=_F32) / N_HEADS)

    def vec(a):
        return a.reshape(1, -1)

    w_qkv = attn_w_qkv[0]
    qt, k, vt, (w_o, w_gate, w_up, w_down, w_in, w_out) = _qkv_proj(
        x2d, w_qkv[:, :D_MODEL].T.astype(_BF16), w_qkv[:, D_MODEL:2 * D_MODEL].astype(_BF16),
        w_qkv[:, 2 * D_MODEL:].T.astype(_BF16), bsz, seq,
        [attn_w_o, ffn_w_gate, ffn_w_up, ffn_w_down, gmlp_w_in, gmlp_w_out])
    lam_params = jnp.stack([attn_lambda_q1[0], attn_lambda_k1[0],
                            attn_lambda_q2[0], attn_lambda_k2[0]])
    o = _attention(qt, k.reshape(bsz, seq, d), vt,
                   slopes, lam_params, vec(attn_subln_g[0]), _lambda_init(0))
    x2d = _ffn(x2d, w_gate, w_up, w_down, 0, vec(ln_ffn_g[0]), vec(ln_ffn_b[0]),
               out_proj=(o.reshape(m, d), w_o, vec(ln_mix_g[0]), vec(ln_mix_b[0])))

    x2d = _gmlp(x2d, w_in[0], vec(gmlp_b_in[0]), vec(gmlp_ln_g[0]),
                vec(gmlp_ln_b[0]), gmlp_w_s[0], gmlp_b_s[0].T, w_out[0],
                vec(ln_mix_g[1]), vec(ln_mix_b[1]))
    x2d = _ffn(x2d, w_gate, w_up, w_down, 1, vec(ln_ffn_g[1]), vec(ln_ffn_b[1]))
    return x2d.reshape(bsz, seq, d)
```

```python
import functools
import math
import struct

import jax
import jax.numpy as jnp
from jax import lax
from jax.experimental import pallas as pl
from jax.experimental.pallas import tpu as pltpu

D_MODEL = 1024
DEPTH = 2
CHUNK = 64
N_HEADS = 8
HEAD_DIM_QK = 64
HEAD_DIM_V = 128
GMLP_BLOCK = 128
GMLP_HALF = 2 * D_MODEL
GMLP_GROUPS = 8
GMLP_GROUP_DIM = GMLP_HALF // GMLP_GROUPS
D_FF = 2816
DEEPNORM_ALPHA = (2 * DEPTH) ** 0.25
LN_EPS = 1e-5
MASK_VALUE = -1e30
QK_SCALE = HEAD_DIM_QK ** -0.5
LOG2E = math.log2(math.e)

V7X_VMEM_LIMIT_BYTES = 56 * 1024 * 1024
BF16_SUBLANES = 16

PROJ_TM = 512
FFN_TM = 512
GMLP_TM = 256
ATTN_T = 256

_BF16 = jnp.bfloat16
_F32 = jnp.float32


def _bf16_round(x):
    bits = struct.unpack("<I", struct.pack("<f", x))[0]
    bits = (bits + 0x7FFF + ((bits >> 16) & 1)) & 0xFFFF0000
    return struct.unpack("<f", struct.pack("<I", bits))[0]


def _bf16_pieces(x, n=3):
    pieces = []
    for _ in range(n):
        p = _bf16_round(x)
        pieces.append(p)
        x -= p
    return pieces


def _lambda_init(layer_idx):
    return 0.8 - 0.6 * math.exp(-0.3 * layer_idx)


def _layernorm(y, g, b):
    mu = jnp.mean(y, axis=-1, keepdims=True)
    d = y - mu
    var = jnp.mean(d * d, axis=-1, keepdims=True)
    return d * lax.rsqrt(var + LN_EPS) * g + b


def _dot(a, b):
    return jnp.dot(a, b, preferred_element_type=_F32)


def _dot_nt(a, b):
    return lax.dot_general(a, b, (((1,), (1,)), ((), ())), preferred_element_type=_F32)


def _resident(shape):
    return pl.BlockSpec(shape, lambda *_: (0,) * len(shape), pipeline_mode=pl.Buffered(1))


def _params(n_parallel_axes=1):
    return pltpu.CompilerParams(
        dimension_semantics=("arbitrary",) * n_parallel_axes,
        vmem_limit_bytes=V7X_VMEM_LIMIT_BYTES,
    )


def _qkv_kernel(x_ref, wqt_ref, wk_ref, wvt_ref, *refs, n_cast):
    cast_src = refs[:n_cast]
    qt_ref, k_ref, vt_ref = refs[n_cast:n_cast + 3]
    cast_dst = refs[n_cast + 3:]
    xb = x_ref[...].astype(_BF16)
    qt_ref[0] = (_dot_nt(wqt_ref[...], xb) * (QK_SCALE * LOG2E)).astype(_BF16)
    k_ref[...] = _dot(xb, wk_ref[...]).astype(_BF16)
    vt_ref[0] = _dot_nt(wvt_ref[...], xb).astype(_BF16)
    for src, dst in zip(cast_src, cast_dst):
        dst[...] = src[...].astype(_BF16)


def _qkv_proj(x2d, w_q_t, w_k, w_v_t, bsz, seq, later_weights):
    m = x2d.shape[0]
    steps = m // PROJ_TM
    tiles_per_seq = seq // PROJ_TM
    row_spec = pl.BlockSpec((PROJ_TM, D_MODEL), lambda i: (i, 0))
    t_spec = pl.BlockSpec((1, D_MODEL, PROJ_TM),
                          lambda i: (i // tiles_per_seq, 0, i % tiles_per_seq))
    w_spec = _resident((D_MODEL, D_MODEL))
    t_shape = jax.ShapeDtypeStruct((bsz, D_MODEL, seq), _BF16)

    flat = [w.reshape(-1, w.shape[-1]) for w in later_weights]
    for w in flat:
        assert w.shape[0] % (steps * BF16_SUBLANES) == 0, w.shape
    cast_specs = [pl.BlockSpec((w.shape[0] // steps, w.shape[1]), lambda i: (i, 0)) for w in flat]
    cast_shapes = [jax.ShapeDtypeStruct(w.shape, _BF16) for w in flat]

    outs = pl.pallas_call(
        functools.partial(_qkv_kernel, n_cast=len(flat)),
        grid=(steps,),
        in_specs=[row_spec, w_spec, w_spec, w_spec] + cast_specs,
        out_specs=[t_spec, row_spec, t_spec] + cast_specs,
        out_shape=[t_shape, jax.ShapeDtypeStruct((m, D_MODEL), _BF16), t_shape] + cast_shapes,
        compiler_params=_params(),
        name="qkv_proj",
    )(x2d, w_q_t, w_k, w_v_t, *flat)
    cast = [c.reshape(w.shape) for c, w in zip(outs[3:], later_weights)]
    return outs[0], outs[1], outs[2], cast


_LOG2E_PIECES = _bf16_pieces(LOG2E)


def _attn_kernel(slopes_ref, lam_ref, qt_ref, k_ref, vt_ref, g_ref, o_ref,
                 kaug_ref, qs_ref, s_ref, *, lambda_init):
    T = ATTN_T
    DQ = 2 * HEAD_DIM_QK
    seq = k_ref.shape[1]
    n_tiles = seq // T
    h = pl.program_id(1)
    slope = slopes_ref[h]

    pos = lax.broadcasted_iota(jnp.int32, (seq, DQ), 0)
    lane_k = lax.broadcasted_iota(jnp.int32, (seq, DQ), 1)
    j_hi = (pos // CHUNK).astype(_F32) * slope
    j_lo = (pos % CHUNK).astype(_F32) * slope
    key_aug = jnp.where(lane_k < 3, j_hi, jnp.where(lane_k < 6, j_lo, 0.0))
    kaug_ref[:, 0:DQ] = k_ref[0]
    kaug_ref[:, DQ:2 * DQ] = key_aug.astype(_BF16)

    row_q = lax.broadcasted_iota(jnp.int32, (DQ, 2 * T), 0)
    query_aug = jnp.zeros((DQ, 2 * T), _F32)
    for idx, piece in enumerate(_LOG2E_PIECES):
        query_aug = jnp.where(row_q == idx, CHUNK * piece, query_aug)
        query_aug = jnp.where(row_q == idx + 3, piece, query_aug)
    query_aug = query_aug.astype(_BF16)

    lp = lam_ref[...]
    lam = (jnp.exp(jnp.sum(lp[0:1] * lp[1:2], axis=-1, keepdims=True))
           - jnp.exp(jnp.sum(lp[2:3] * lp[3:4], axis=-1, keepdims=True))
           + lambda_init)

    key_i = lax.broadcasted_iota(jnp.int32, (T, T), 0)
    qry_i = lax.broadcasted_iota(jnp.int32, (T, T), 1)
    ahead = jnp.maximum(key_i - qry_i, 0).astype(_F32)
    allowed = (key_i // CHUNK) <= (qry_i // CHUNK)
    diag_add = jnp.where(allowed, (-2.0 * LOG2E * slope) * ahead, MASK_VALUE)
    diag_add = jnp.concatenate([diag_add, diag_add], axis=1)

    row_d = lax.broadcasted_iota(jnp.int32, (DQ, T), 0)
    g_scaled = g_ref[...] * (1.0 - lambda_init)

    def stage_queries(t):
        qt = qt_ref[0, :, t * T:(t + 1) * T]
        zero = jnp.zeros_like(qt)
        top = jnp.concatenate([jnp.where(row_d < HEAD_DIM_QK, qt, zero),
                               jnp.where(row_d >= HEAD_DIM_QK, qt, zero)], axis=1)
        qs_ref[t % 2] = jnp.concatenate([top, query_aug], axis=0)

    def score_pass(t, j, m):
        rows = slice(j * T, (j + 1) * T)
        s = _dot(kaug_ref[rows, :], qs_ref[t % 2])
        if j == t:
            s = s + diag_add
        s_ref[t % 2, rows, :] = s
        m_blk = jnp.max(s, axis=0, keepdims=True)
        return m_blk if m is None else jnp.maximum(m, m_blk)

    def value_pass(t, j, m, l, acc):
        rows = slice(j * T, (j + 1) * T)
        p = jnp.exp2(s_ref[t % 2, rows, :] - m)
        l_blk = jnp.sum(p, axis=0, keepdims=True)
        acc_blk = _dot(vt_ref[0, :, rows], p.astype(_BF16))
        if l is None:
            return l_blk, acc_blk
        return l + l_blk, acc + acc_blk

    def finish(t, l, acc):
        att = acc / l
        o = (att[:, 0:T] - lam * att[:, T:2 * T]).T
        ms = jnp.mean(o * o, axis=-1, keepdims=True)
        o = o * lax.rsqrt(ms + LN_EPS) * g_scaled
        o_ref[0, t * T:(t + 1) * T, :] = o.astype(_BF16)

    m_prev = None
    for t in range(n_tiles + 1):
        m_cur, l, acc = None, None, None
        if t < n_tiles:
            stage_queries(t)
        for j in range(t + 1):
            if t < n_tiles:
                m_cur = score_pass(t, j, m_cur)
            if t > 0 and j < t:
                l, acc = value_pass(t - 1, j, m_prev, l, acc)
        if t > 0:
            finish(t - 1, l, acc)
        m_prev = m_cur


def _attention(qt, k, vt, slopes, lam_params, g_sub, lambda_init):
    b, s, _ = k.shape
    head_spec = pl.BlockSpec((1, s, HEAD_DIM_V), lambda bi, hi: (bi, 0, hi))
    head_t_spec = pl.BlockSpec((1, HEAD_DIM_V, s), lambda bi, hi: (bi, hi, 0))
    return pl.pallas_call(
        functools.partial(_attn_kernel, lambda_init=lambda_init),
        grid=(b, N_HEADS),
        in_specs=[
            pl.BlockSpec(memory_space=pltpu.SMEM),
            pl.BlockSpec((4, HEAD_DIM_QK), lambda bi, hi: (0, 0)),
            head_t_spec, head_spec, head_t_spec,
            pl.BlockSpec((1, HEAD_DIM_V), lambda bi, hi: (0, 0)),
        ],
        out_specs=head_spec,
        out_shape=jax.ShapeDtypeStruct((b, s, N_HEADS * HEAD_DIM_V), _BF16),
        scratch_shapes=[
            pltpu.VMEM((s, 4 * HEAD_DIM_QK), _BF16),
            pltpu.VMEM((2, 4 * HEAD_DIM_QK, 2 * ATTN_T), _BF16),
            pltpu.VMEM((2, s, 2 * ATTN_T), _F32),
        ],
        compiler_params=_params(2),
        name="diff_attention",
    )(slopes, lam_params, qt, k, vt, g_sub)


def _ffn_kernel(*refs, with_out_proj):
    if with_out_proj:
        o_ref, wo_ref, mix_g_ref, mix_b_ref = refs[:4]
        refs = refs[4:]
    x_ref, wg_ref, wu_ref, wd_ref, g_ref, b_ref, y_ref = refs
    x = x_ref[...]
    if with_out_proj:
        mix = _dot(o_ref[...], wo_ref[0])
        x = _layernorm(DEEPNORM_ALPHA * x + mix, mix_g_ref[...], mix_b_ref[...])
    xb = x.astype(_BF16)
    gate = _dot(xb, wg_ref[0])
    up = _dot(xb, wu_ref[0])
    act = (gate * jax.nn.sigmoid(gate) * up).astype(_BF16)
    y = _dot(act, wd_ref[0])
    y_ref[...] = _layernorm(DEEPNORM_ALPHA * x + y, g_ref[...], b_ref[...])


def _layer_weight(shape, layer):
    return pl.BlockSpec((1,) + shape, lambda i: (layer, 0, 0), pipeline_mode=pl.Buffered(1))


def _ffn(x2d, w_gate, w_up, w_down, layer, g, b, out_proj=None):
    m = x2d.shape[0]
    row_spec = pl.BlockSpec((FFN_TM, D_MODEL), lambda i: (i, 0))
    vec_spec = pl.BlockSpec((1, D_MODEL), lambda i: (0, 0))
    in_specs = [row_spec, _layer_weight((D_MODEL, D_FF), layer),
                _layer_weight((D_MODEL, D_FF), layer), _layer_weight((D_FF, D_MODEL), layer),
                vec_spec, vec_spec]
    args = [x2d, w_gate, w_up, w_down, g, b]
    if out_proj is not None:
        in_specs = [row_spec, _layer_weight((D_MODEL, D_MODEL), 0), vec_spec, vec_spec] + in_specs
        args = list(out_proj) + args
    return pl.pallas_call(
        functools.partial(_ffn_kernel, with_out_proj=out_proj is not None),
        grid=(m // FFN_TM,),
        in_specs=in_specs,
        out_specs=row_spec,
        out_shape=jax.ShapeDtypeStruct((m, D_MODEL), _F32),
        compiler_params=_params(),
        name="ffn_ln",
    )(*args)


def _gmlp_kernel(x_ref, xnext_ref, win_ref, bin_ref, lng_ref, lnb_ref, ws_ref, bs_ref,
                 wout_ref, g_ref, b_ref, y_ref, za_ref, zb_ref, gated_a_ref, gated_b_ref):
    tm = GMLP_TM
    row = lax.broadcasted_iota(jnp.int32, (GMLP_BLOCK, GMLP_BLOCK), 0)
    col = lax.broadcasted_iota(jnp.int32, (GMLP_BLOCK, GMLP_BLOCK), 1)
    tri = (row >= col).astype(_F32)
    bs = bs_ref[...]

    def stage1(x, z_ref):
        z = _dot(x.astype(_BF16), win_ref[...]) + bin_ref[...]
        z_ref[...] = 0.5 * z * (1.0 + lax.erf(z * (2.0 ** -0.5)))

    def stage2(x, z_ref, gated_ref, out_rows):
        v = _layernorm(z_ref[:, GMLP_HALF:], lng_ref[...], lnb_ref[...]).astype(_BF16)
        for grp in range(GMLP_GROUPS):
            w_mix = (ws_ref[grp] * tri).astype(_BF16)
            cols = slice(grp * GMLP_GROUP_DIM, (grp + 1) * GMLP_GROUP_DIM)
            for blk in range(tm // GMLP_BLOCK):
                rows = slice(blk * GMLP_BLOCK, (blk + 1) * GMLP_BLOCK)
                mixed = _dot(w_mix, v[rows, cols]) + bs[:, grp:grp + 1]
                gated_ref[rows, cols] = (z_ref[rows, cols] * mixed).astype(_BF16)
        y = _dot(gated_ref[...], wout_ref[...])
        y_ref[out_rows, :] = _layernorm(DEEPNORM_ALPHA * x + y, g_ref[...], b_ref[...])

    @pl.when(pl.program_id(0) == 0)
    def _():
        stage1(x_ref[0:tm, :], za_ref)

    stage1(x_ref[tm:2 * tm, :], zb_ref)
    stage2(x_ref[0:tm, :], za_ref, gated_a_ref, slice(0, tm))
    stage1(xnext_ref[...], za_ref)
    stage2(x_ref[tm:2 * tm, :], zb_ref, gated_b_ref, slice(tm, 2 * tm))


def _gmlp(x2d, w_in, b_in, ln_g, ln_b, w_s, b_s_t, w_out, g, b):
    m = x2d.shape[0]
    steps = m // (2 * GMLP_TM)
    pair_spec = pl.BlockSpec((2 * GMLP_TM, D_MODEL), lambda i: (i, 0))
    next_spec = pl.BlockSpec((GMLP_TM, D_MODEL),
                             lambda i: (jnp.minimum(2 * i + 2, 2 * steps - 1), 0))
    vec_spec = pl.BlockSpec((1, D_MODEL), lambda i: (0, 0))
    half_spec = pl.BlockSpec((1, GMLP_HALF), lambda i: (0, 0))
    return pl.pallas_call(
        _gmlp_kernel,
        grid=(steps,),
        in_specs=[pair_spec, next_spec,
                  _resident((D_MODEL, 2 * GMLP_HALF)),
                  pl.BlockSpec((1, 2 * GMLP_HALF), lambda i: (0, 0)),
                  half_spec, half_spec,
                  _resident((GMLP_GROUPS, GMLP_BLOCK, GMLP_BLOCK)),
                  pl.BlockSpec((GMLP_BLOCK, GMLP_GROUPS), lambda i: (0, 0)),
                  _resident((GMLP_HALF, D_MODEL)),
                  vec_spec, vec_spec],
        out_specs=pair_spec,
        out_shape=jax.ShapeDtypeStruct((m, D_MODEL), _F32),
        scratch_shapes=[pltpu.VMEM((GMLP_TM, 2 * GMLP_HALF), _F32),
                        pltpu.VMEM((GMLP_TM, 2 * GMLP_HALF), _F32),
                        pltpu.VMEM((GMLP_TM, GMLP_HALF), _BF16),
                        pltpu.VMEM((GMLP_TM, GMLP_HALF), _BF16)],
        compiler_params=_params(),
        name="gmlp_ln",
    )(x2d, x2d, w_in, b_in, ln_g, ln_b, w_s, b_s_t, w_out, g, b)


def kernel(x, attn_w_qkv, attn_lambda_q1, attn_lambda_k1, attn_lambda_q2, attn_lambda_k2,
           attn_subln_g, attn_w_o, gmlp_w_in, gmlp_b_in, gmlp_ln_g, gmlp_ln_b, gmlp_w_s,
           gmlp_b_s, gmlp_w_out, ln_mix_g, ln_mix_b, ffn_w_gate, ffn_w_up, ffn_w_down,
           ln_ffn_g, ln_ffn_b):
    bsz, seq, d = x.shape
    assert d == D_MODEL and seq % PROJ_TM == 0 and seq % ATTN_T == 0
    m = bsz * seq
    x2d = x.reshape(m, d)
    slopes = 2.0 ** (-8.0 * jnp.arange(1, N_HEADS + 1, dtype=_F32) / N_HEADS)

    def vec(a):
        return a.reshape(1, -1)

    w_qkv = attn_w_qkv[0]
    qt, k, vt, (w_o, w_gate, w_up, w_down, w_in, w_out) = _qkv_proj(
        x2d, w_qkv[:, :D_MODEL].T.astype(_BF16), w_qkv[:, D_MODEL:2 * D_MODEL].astype(_BF16),
        w_qkv[:, 2 * D_MODEL:].T.astype(_BF16), bsz, seq,
        [attn_w_o, ffn_w_gate, ffn_w_up, ffn_w_down, gmlp_w_in, gmlp_w_out])
    lam_params = jnp.stack([attn_lambda_q1[0], attn_lambda_k1[0],
                            attn_lambda_q2[0], attn_lambda_k2[0]])
    o = _attention(qt, k.reshape(bsz, seq, d), vt,
                   slopes, lam_params, vec(attn_subln_g[0]), _lambda_init(0))
    x2d = _ffn(x2d, w_gate, w_up, w_down, 0, vec(ln_ffn_g[0]), vec(ln_ffn_b[0]),
               out_proj=(o.reshape(m, d), w_o, vec(ln_mix_g[0]), vec(ln_mix_b[0])))

    x2d = _gmlp(x2d, w_in[0], vec(gmlp_b_in[0]), vec(gmlp_ln_g[0]),
                vec(gmlp_ln_b[0]), gmlp_w_s[0], gmlp_b_s[0].T, w_out[0],
                vec(ln_mix_g[1]), vec(ln_mix_b[1]))
    x2d = _ffn(x2d, w_gate, w_up, w_down, 1, vec(ln_ffn_g[1]), vec(ln_ffn_b[1]))
    return x2d.reshape(bsz, seq, d)
```

```python
import functools
import math
import struct

import jax
import jax.numpy as jnp
from jax import lax
from jax.experimental import pallas as pl
from jax.experimental.pallas import tpu as pltpu

D_MODEL = 1024
DEPTH = 2
CHUNK = 64
N_HEADS = 8
HEAD_DIM_QK = 64
HEAD_DIM_V = 128
GMLP_BLOCK = 128
GMLP_HALF = 2 * D_MODEL
GMLP_GROUPS = 8
GMLP_GROUP_DIM = GMLP_HALF // GMLP_GROUPS
D_FF = 2816
DEEPNORM_ALPHA = (2 * DEPTH) ** 0.25
LN_EPS = 1e-5
MASK_VALUE = -1e30
QK_SCALE = HEAD_DIM_QK ** -0.5
LOG2E = math.log2(math.e)

V7X_VMEM_LIMIT_BYTES = 56 * 1024 * 1024
BF16_SUBLANES = 16

PROJ_TM = 512
FFN_TM = 512
GMLP_TM = 256
ATTN_T = 256
SCORE_SLOTS = 2

_BF16 = jnp.bfloat16
_F32 = jnp.float32


def _bf16_round(x):
    bits = struct.unpack("<I", struct.pack("<f", x))[0]
    bits = (bits + 0x7FFF + ((bits >> 16) & 1)) & 0xFFFF0000
    return struct.unpack("<f", struct.pack("<I", bits))[0]


def _bf16_pieces(x, n=3):
    pieces = []
    for _ in range(n):
        p = _bf16_round(x)
        pieces.append(p)
        x -= p
    return pieces


def _lambda_init(layer_idx):
    return 0.8 - 0.6 * math.exp(-0.3 * layer_idx)


def _layernorm(y, g, b):
    mu = jnp.mean(y, axis=-1, keepdims=True)
    d = y - mu
    var = jnp.mean(d * d, axis=-1, keepdims=True)
    return d * lax.rsqrt(var + LN_EPS) * g + b


def _dot(a, b):
    return jnp.dot(a, b, preferred_element_type=_F32)


def _dot_nt(a, b):
    return lax.dot_general(a, b, (((1,), (1,)), ((), ())), preferred_element_type=_F32)


def _resident(shape):
    return pl.BlockSpec(shape, lambda *_: (0,) * len(shape), pipeline_mode=pl.Buffered(1))


def _params(n_parallel_axes=1, flags=None):
    return pltpu.CompilerParams(
        dimension_semantics=("arbitrary",) * n_parallel_axes,
        vmem_limit_bytes=V7X_VMEM_LIMIT_BYTES,
        flags=flags,
    )


def _qkv_kernel(x_ref, wqt_ref, wk_ref, wvt_ref, *refs, n_cast):
    cast_src = refs[:n_cast]
    qt_ref, k_ref, vt_ref = refs[n_cast:n_cast + 3]
    cast_dst = refs[n_cast + 3:]
    xb = x_ref[...].astype(_BF16)
    qt_ref[0] = (_dot_nt(wqt_ref[...], xb) * (QK_SCALE * LOG2E)).astype(_BF16)
    k_ref[...] = _dot(xb, wk_ref[...]).astype(_BF16)
    vt_ref[0] = _dot_nt(wvt_ref[...], xb).astype(_BF16)
    for src, dst in zip(cast_src, cast_dst):
        dst[...] = src[...].astype(_BF16)


def _qkv_proj(x2d, w_q_t, w_k, w_v_t, bsz, seq, later_weights):
    m = x2d.shape[0]
    steps = m // PROJ_TM
    tiles_per_seq = seq // PROJ_TM
    row_spec = pl.BlockSpec((PROJ_TM, D_MODEL), lambda i: (i, 0))
    t_spec = pl.BlockSpec((1, D_MODEL, PROJ_TM),
                          lambda i: (i // tiles_per_seq, 0, i % tiles_per_seq))
    w_spec = _resident((D_MODEL, D_MODEL))
    t_shape = jax.ShapeDtypeStruct((bsz, D_MODEL, seq), _BF16)

    flat = [w.reshape(-1, w.shape[-1]) for w in later_weights]
    for w in flat:
        assert w.shape[0] % (steps * BF16_SUBLANES) == 0, w.shape
    cast_specs = [pl.BlockSpec((w.shape[0] // steps, w.shape[1]), lambda i: (i, 0)) for w in flat]
    cast_shapes = [jax.ShapeDtypeStruct(w.shape, _BF16) for w in flat]

    outs = pl.pallas_call(
        functools.partial(_qkv_kernel, n_cast=len(flat)),
        grid=(steps,),
        in_specs=[row_spec, w_spec, w_spec, w_spec] + cast_specs,
        out_specs=[t_spec, row_spec, t_spec] + cast_specs,
        out_shape=[t_shape, jax.ShapeDtypeStruct((m, D_MODEL), _BF16), t_shape] + cast_shapes,
        compiler_params=_params(),
        name="qkv_proj",
    )(x2d, w_q_t, w_k, w_v_t, *flat)
    cast = [c.reshape(w.shape) for c, w in zip(outs[3:], later_weights)]
    return outs[0], outs[1], outs[2], cast


_LOG2E_PIECES = _bf16_pieces(LOG2E)


def _attn_kernel(slopes_ref, lam_ref, qt_ref, k_ref, vt_ref, g_ref, o_ref,
                 kaug_ref, qs_ref, s_ref, *, lambda_init):
    T = ATTN_T
    DQ = 2 * HEAD_DIM_QK
    seq = k_ref.shape[1]
    n_tiles = seq // T
    h = pl.program_id(1)
    slope = slopes_ref[h]

    pos = lax.broadcasted_iota(jnp.int32, (seq, DQ), 0)
    lane_k = lax.broadcasted_iota(jnp.int32, (seq, DQ), 1)
    j_hi = (pos // CHUNK).astype(_F32) * slope
    j_lo = (pos % CHUNK).astype(_F32) * slope
    key_aug = jnp.where(lane_k < 3, j_hi, jnp.where(lane_k < 6, j_lo, 0.0))
    kaug_ref[:, 0:DQ] = k_ref[0]
    kaug_ref[:, DQ:2 * DQ] = key_aug.astype(_BF16)

    row_q = lax.broadcasted_iota(jnp.int32, (DQ, 2 * T), 0)
    query_aug = jnp.zeros((DQ, 2 * T), _F32)
    for idx, piece in enumerate(_LOG2E_PIECES):
        query_aug = jnp.where(row_q == idx, CHUNK * piece, query_aug)
        query_aug = jnp.where(row_q == idx + 3, piece, query_aug)
    query_aug = query_aug.astype(_BF16)

    lp = lam_ref[...]
    lam = (jnp.exp(jnp.sum(lp[0:1] * lp[1:2], axis=-1, keepdims=True))
           - jnp.exp(jnp.sum(lp[2:3] * lp[3:4], axis=-1, keepdims=True))
           + lambda_init)

    key_i = lax.broadcasted_iota(jnp.int32, (T, T), 0)
    qry_i = lax.broadcasted_iota(jnp.int32, (T, T), 1)
    ahead = jnp.maximum(key_i - qry_i, 0).astype(_F32)
    allowed = (key_i // CHUNK) <= (qry_i // CHUNK)
    diag_add = jnp.where(allowed, (-2.0 * LOG2E * slope) * ahead, MASK_VALUE)
    diag_add = jnp.concatenate([diag_add, diag_add], axis=1)

    row_d = lax.broadcasted_iota(jnp.int32, (DQ, T), 0)
    g_scaled = g_ref[...] * (1.0 - lambda_init)

    def stage_queries(t):
        qt = qt_ref[0, :, t * T:(t + 1) * T]
        zero = jnp.zeros_like(qt)
        top = jnp.concatenate([jnp.where(row_d < HEAD_DIM_QK, qt, zero),
                               jnp.where(row_d >= HEAD_DIM_QK, qt, zero)], axis=1)
        qs_ref[t % SCORE_SLOTS] = jnp.concatenate([top, query_aug], axis=0)

    def score_pass(t, j, m):
        rows = slice(j * T, (j + 1) * T)
        s = jnp.dot(kaug_ref[rows, :], qs_ref[t % SCORE_SLOTS],
                    preferred_element_type=_F32)
        if j == t:
            s = s + diag_add
        s_ref[t % SCORE_SLOTS, rows, :] = s
        m_blk = jnp.max(s, axis=0, keepdims=True)
        return m_blk if m is None else jnp.maximum(m, m_blk)

    def value_pass(t, j, m, l, acc):
        rows = slice(j * T, (j + 1) * T)
        p = jnp.exp2(s_ref[t % SCORE_SLOTS, rows, :] - m)
        l_blk = jnp.sum(p, axis=0, keepdims=True)
        acc_blk = jnp.dot(vt_ref[0, :, rows], p.astype(_BF16),
                          preferred_element_type=_F32)
        if l is None:
            return l_blk, acc_blk
        return l + l_blk, acc + acc_blk

    def finish(t, l, acc):
        att = acc / l
        o = (att[:, 0:T] - lam * att[:, T:2 * T]).T
        ms = jnp.mean(o * o, axis=-1, keepdims=True)
        o = o * lax.rsqrt(ms + LN_EPS) * g_scaled
        o_ref[0, t * T:(t + 1) * T, :] = o.astype(_BF16)

    order = list(range(n_tiles - 1, -1, -1))
    col_max = {}
    for i in range(n_tiles + 1):
        if i < n_tiles:
            t = order[i]
            stage_queries(t)
            m = None
            for j in range(t + 1):
                m = score_pass(t, j, m)
            col_max[t] = m
        if i > 0:
            t = order[i - 1]
            l, acc = None, None
            for j in range(t + 1):
                l, acc = value_pass(t, j, col_max[t], l, acc)
            finish(t, l, acc)


def _attention(qt, k, vt, slopes, lam_params, g_sub, lambda_init):
    b, s, _ = k.shape
    head_spec = pl.BlockSpec((1, s, HEAD_DIM_V), lambda bi, hi: (bi, 0, hi))
    head_t_spec = pl.BlockSpec((1, HEAD_DIM_V, s), lambda bi, hi: (bi, hi, 0))
    return pl.pallas_call(
        functools.partial(_attn_kernel, lambda_init=lambda_init),
        grid=(b, N_HEADS),
        in_specs=[
            pl.BlockSpec(memory_space=pltpu.SMEM),
            pl.BlockSpec((4, HEAD_DIM_QK), lambda bi, hi: (0, 0)),
            head_t_spec, head_spec, head_t_spec,
            pl.BlockSpec((1, HEAD_DIM_V), lambda bi, hi: (0, 0)),
        ],
        out_specs=head_spec,
        out_shape=jax.ShapeDtypeStruct((b, s, N_HEADS * HEAD_DIM_V), _BF16),
        scratch_shapes=[
            pltpu.VMEM((s, 4 * HEAD_DIM_QK), _BF16),
            pltpu.VMEM((SCORE_SLOTS, 4 * HEAD_DIM_QK, 2 * ATTN_T), _BF16),
            pltpu.VMEM((SCORE_SLOTS, s, 2 * ATTN_T), _F32),
        ],
        compiler_params=_params(2),
        name="diff_attention",
    )(slopes, lam_params, qt, k, vt, g_sub)


def _ffn_kernel(*refs, with_out_proj):
    if with_out_proj:
        o_ref, wo_ref, mix_g_ref, mix_b_ref = refs[:4]
        refs = refs[4:]
    x_ref, wg_ref, wu_ref, wd_ref, g_ref, b_ref, y_ref = refs
    x = x_ref[...]
    if with_out_proj:
        mix = _dot(o_ref[...], wo_ref[0])
        x = _layernorm(DEEPNORM_ALPHA * x + mix, mix_g_ref[...], mix_b_ref[...])
    xb = x.astype(_BF16)
    gate = _dot(xb, wg_ref[0])
    up = _dot(xb, wu_ref[0])
    act = (gate * jax.nn.sigmoid(gate) * up).astype(_BF16)
    y = _dot(act, wd_ref[0])
    y_ref[...] = _layernorm(DEEPNORM_ALPHA * x + y, g_ref[...], b_ref[...])


def _layer_weight(shape, layer):
    return pl.BlockSpec((1,) + shape, lambda i: (layer, 0, 0), pipeline_mode=pl.Buffered(1))


def _ffn(x2d, w_gate, w_up, w_down, layer, g, b, out_proj=None):
    m = x2d.shape[0]
    row_spec = pl.BlockSpec((FFN_TM, D_MODEL), lambda i: (i, 0))
    vec_spec = pl.BlockSpec((1, D_MODEL), lambda i: (0, 0))
    in_specs = [row_spec, _layer_weight((D_MODEL, D_FF), layer),
                _layer_weight((D_MODEL, D_FF), layer), _layer_weight((D_FF, D_MODEL), layer),
                vec_spec, vec_spec]
    args = [x2d, w_gate, w_up, w_down, g, b]
    if out_proj is not None:
        in_specs = [row_spec, _layer_weight((D_MODEL, D_MODEL), 0), vec_spec, vec_spec] + in_specs
        args = list(out_proj) + args
    return pl.pallas_call(
        functools.partial(_ffn_kernel, with_out_proj=out_proj is not None),
        grid=(m // FFN_TM,),
        in_specs=in_specs,
        out_specs=row_spec,
        out_shape=jax.ShapeDtypeStruct((m, D_MODEL), _F32),
        compiler_params=_params(),
        name="ffn_ln",
    )(*args)


def _gmlp_kernel(x_ref, xnext_ref, win_ref, bin_ref, lng_ref, lnb_ref, ws_ref, bs_ref,
                 wout_ref, g_ref, b_ref, y_ref, za_ref, zb_ref, gated_a_ref, gated_b_ref):
    tm = GMLP_TM
    row = lax.broadcasted_iota(jnp.int32, (GMLP_BLOCK, GMLP_BLOCK), 0)
    col = lax.broadcasted_iota(jnp.int32, (GMLP_BLOCK, GMLP_BLOCK), 1)
    tri = (row >= col).astype(_F32)
    bs = bs_ref[...]

    def stage1(x, z_ref):
        z = _dot(x.astype(_BF16), win_ref[...]) + bin_ref[...]
        z_ref[...] = 0.5 * z * (1.0 + lax.erf(z * (2.0 ** -0.5)))

    def stage2(x, z_ref, gated_ref, out_rows):
        v = _layernorm(z_ref[:, GMLP_HALF:], lng_ref[...], lnb_ref[...]).astype(_BF16)
        for grp in range(GMLP_GROUPS):
            w_mix = (ws_ref[grp] * tri).astype(_BF16)
            cols = slice(grp * GMLP_GROUP_DIM, (grp + 1) * GMLP_GROUP_DIM)
            for blk in range(tm // GMLP_BLOCK):
                rows = slice(blk * GMLP_BLOCK, (blk + 1) * GMLP_BLOCK)
                mixed = _dot(w_mix, v[rows, cols]) + bs[:, grp:grp + 1]
                gated_ref[rows, cols] = (z_ref[rows, cols] * mixed).astype(_BF16)
        y = _dot(gated_ref[...], wout_ref[...])
        y_ref[out_rows, :] = _layernorm(DEEPNORM_ALPHA * x + y, g_ref[...], b_ref[...])

    @pl.when(pl.program_id(0) == 0)
    def _():
        stage1(x_ref[0:tm, :], za_ref)

    stage1(x_ref[tm:2 * tm, :], zb_ref)
    stage2(x_ref[0:tm, :], za_ref, gated_a_ref, slice(0, tm))
    stage1(xnext_ref[...], za_ref)
    stage2(x_ref[tm:2 * tm, :], zb_ref, gated_b_ref, slice(tm, 2 * tm))


def _gmlp(x2d, w_in, b_in, ln_g, ln_b, w_s, b_s_t, w_out, g, b):
    m = x2d.shape[0]
    steps = m // (2 * GMLP_TM)
    pair_spec = pl.BlockSpec((2 * GMLP_TM, D_MODEL), lambda i: (i, 0))
    next_spec = pl.BlockSpec((GMLP_TM, D_MODEL),
                             lambda i: (jnp.minimum(2 * i + 2, 2 * steps - 1), 0))
    vec_spec = pl.BlockSpec((1, D_MODEL), lambda i: (0, 0))
    half_spec = pl.BlockSpec((1, GMLP_HALF), lambda i: (0, 0))
    return pl.pallas_call(
        _gmlp_kernel,
        grid=(steps,),
        in_specs=[pair_spec, next_spec,
                  _resident((D_MODEL, 2 * GMLP_HALF)),
                  pl.BlockSpec((1, 2 * GMLP_HALF), lambda i: (0, 0)),
                  half_spec, half_spec,
                  _resident((GMLP_GROUPS, GMLP_BLOCK, GMLP_BLOCK)),
                  pl.BlockSpec((GMLP_BLOCK, GMLP_GROUPS), lambda i: (0, 0)),
                  _resident((GMLP_HALF, D_MODEL)),
                  vec_spec, vec_spec],
        out_specs=pair_spec,
        out_shape=jax.ShapeDtypeStruct((m, D_MODEL), _F32),
        scratch_shapes=[pltpu.VMEM((GMLP_TM, 2 * GMLP_HALF), _F32),
                        pltpu.VMEM((GMLP_TM, 2 * GMLP_HALF), _F32),
                        pltpu.VMEM((GMLP_TM, GMLP_HALF), _BF16),
                        pltpu.VMEM((GMLP_TM, GMLP_HALF), _BF16)],
        compiler_params=_params(),
        name="gmlp_ln",
    )(x2d, x2d, w_in, b_in, ln_g, ln_b, w_s, b_s_t, w_out, g, b)


def kernel(x, attn_w_qkv, attn_lambda_q1, attn_lambda_k1, attn_lambda_q2, attn_lambda_k2,
           attn_subln_g, attn_w_o, gmlp_w_in, gmlp_b_in, gmlp_ln_g, gmlp_ln_b, gmlp_w_s,
           gmlp_b_s, gmlp_w_out, ln_mix_g, ln_mix_b, ffn_w_gate, ffn_w_up, ffn_w_down,
           ln_ffn_g, ln_ffn_b):
    bsz, seq, d = x.shape
    assert d == D_MODEL and seq % PROJ_TM == 0 and seq % ATTN_T == 0
    m = bsz * seq
    x2d = x.reshape(m, d)
    slopes = 2.0 ** (-8.0 * jnp.arange(1, N_HEADS + 1, dtype=_F32) / N_HEADS)

    def vec(a):
        return a.reshape(1, -1)

    w_qkv = attn_w_qkv[0]
    qt, k, vt, (w_o, w_gate, w_up, w_down, w_in, w_out) = _qkv_proj(
        x2d, w_qkv[:, :D_MODEL].T.astype(_BF16), w_qkv[:, D_MODEL:2 * D_MODEL].astype(_BF16),
        w_qkv[:, 2 * D_MODEL:].T.astype(_BF16), bsz, seq,
        [attn_w_o, ffn_w_gate, ffn_w_up, ffn_w_down, gmlp_w_in, gmlp_w_out])
    lam_params = jnp.stack([attn_lambda_q1[0], attn_lambda_k1[0],
                            attn_lambda_q2[0], attn_lambda_k2[0]])
    o = _attention(qt, k.reshape(bsz, seq, d), vt,
                   slopes, lam_params, vec(attn_subln_g[0]), _lambda_init(0))
    x2d = _ffn(x2d, w_gate, w_up, w_down, 0, vec(ln_ffn_g[0]), vec(ln_ffn_b[0]),
               out_proj=(o.reshape(m, d), w_o, vec(ln_mix_g[0]), vec(ln_mix_b[0])))

    x2d = _gmlp(x2d, w_in[0], vec(gmlp_b_in[0]), vec(gmlp_ln_g[0]),
                vec(gmlp_ln_b[0]), gmlp_w_s[0], gmlp_b_s[0].T, w_out[0],
                vec(ln_mix_g[1]), vec(ln_mix_b[1]))
    x2d = _ffn(x2d, w_gate, w_up, w_down, 1, vec(ln_ffn_g[1]), vec(ln_ffn_b[1]))
    return x2d.reshape(bsz, seq, d)
```

```python
import functools
import math
import struct

import jax
import jax.numpy as jnp
from jax import lax
from jax.experimental import pallas as pl
from jax.experimental.pallas import tpu as pltpu

D_MODEL = 1024
DEPTH = 2
CHUNK = 64
N_HEADS = 8
HEAD_DIM_QK = 64
HEAD_DIM_V = 128
GMLP_BLOCK = 128
GMLP_HALF = 2 * D_MODEL
GMLP_GROUPS = 8
GMLP_GROUP_DIM = GMLP_HALF // GMLP_GROUPS
D_FF = 2816
DEEPNORM_ALPHA = (2 * DEPTH) ** 0.25
LN_EPS = 1e-5
MASK_VALUE = -1e30
QK_SCALE = HEAD_DIM_QK ** -0.5
LOG2E = math.log2(math.e)

V7X_VMEM_LIMIT_BYTES = 56 * 1024 * 1024
BF16_SUBLANES = 16

PROJ_TM = 512
FFN_TM = 512
FFN_SPLIT = 2
GMLP_TM = 256
ATTN_T = 256
SCORE_SLOTS = 2

_BF16 = jnp.bfloat16
_F32 = jnp.float32


def _bf16_round(x):
    bits = struct.unpack("<I", struct.pack("<f", x))[0]
    bits = (bits + 0x7FFF + ((bits >> 16) & 1)) & 0xFFFF0000
    return struct.unpack("<f", struct.pack("<I", bits))[0]


def _bf16_pieces(x, n=3):
    pieces = []
    for _ in range(n):
        p = _bf16_round(x)
        pieces.append(p)
        x -= p
    return pieces


def _lambda_init(layer_idx):
    return 0.8 - 0.6 * math.exp(-0.3 * layer_idx)


def _layernorm(y, g, b):
    mu = jnp.mean(y, axis=-1, keepdims=True)
    d = y - mu
    var = jnp.mean(d * d, axis=-1, keepdims=True)
    return d * lax.rsqrt(var + LN_EPS) * g + b


def _dot(a, b):
    return jnp.dot(a, b, preferred_element_type=_F32)


def _dot_nt(a, b):
    return lax.dot_general(a, b, (((1,), (1,)), ((), ())), preferred_element_type=_F32)


def _resident(shape):
    return pl.BlockSpec(shape, lambda *_: (0,) * len(shape), pipeline_mode=pl.Buffered(1))


def _params(n_parallel_axes=1, flags=None):
    return pltpu.CompilerParams(
        dimension_semantics=("arbitrary",) * n_parallel_axes,
        vmem_limit_bytes=V7X_VMEM_LIMIT_BYTES,
        flags=flags,
    )


def _cast_plan(weights, steps, step_of):
    in_specs, out_specs, out_shapes = [], [], []
    for w, layer in weights:
        _, rows, cols = w.shape
        assert rows % (steps * BF16_SUBLANES) == 0, (w.shape, steps)
        block = (1, rows // steps, cols)
        in_specs.append(pl.BlockSpec(block, lambda *g, layer=layer: (layer, step_of(*g), 0)))
        out_specs.append(pl.BlockSpec(block, lambda *g: (0, step_of(*g), 0)))
        out_shapes.append(jax.ShapeDtypeStruct((1, rows, cols), _BF16))
    return in_specs, out_specs, out_shapes


def _run_casts(src_refs, dst_refs):
    for src, dst in zip(src_refs, dst_refs):
        dst[...] = src[...].astype(_BF16)


def _qkv_kernel(x_ref, wqt_ref, wk_ref, wvt_ref, *refs, n_cast):
    cast_src = refs[:n_cast]
    qt_ref, k_ref, vt_ref = refs[n_cast:n_cast + 3]
    cast_dst = refs[n_cast + 3:]
    xb = x_ref[...].astype(_BF16)
    qt_ref[0] = (_dot_nt(wqt_ref[...], xb) * (QK_SCALE * LOG2E)).astype(_BF16)
    k_ref[...] = _dot(xb, wk_ref[...]).astype(_BF16)
    vt_ref[0] = _dot_nt(wvt_ref[...], xb).astype(_BF16)
    _run_casts(cast_src, cast_dst)


def _qkv_proj(x2d, w_q_t, w_k, w_v_t, bsz, seq, cast_weights):
    m = x2d.shape[0]
    steps = m // PROJ_TM
    tiles_per_seq = seq // PROJ_TM
    row_spec = pl.BlockSpec((PROJ_TM, D_MODEL), lambda i: (i, 0))
    t_spec = pl.BlockSpec((1, D_MODEL, PROJ_TM),
                          lambda i: (i // tiles_per_seq, 0, i % tiles_per_seq))
    w_spec = _resident((D_MODEL, D_MODEL))
    t_shape = jax.ShapeDtypeStruct((bsz, D_MODEL, seq), _BF16)
    cast_in, cast_out, cast_shapes = _cast_plan(cast_weights, steps, lambda i: i)

    outs = pl.pallas_call(
        functools.partial(_qkv_kernel, n_cast=len(cast_weights)),
        grid=(steps,),
        in_specs=[row_spec, w_spec, w_spec, w_spec] + cast_in,
        out_specs=[t_spec, row_spec, t_spec] + cast_out,
        out_shape=[t_shape, jax.ShapeDtypeStruct((m, D_MODEL), _BF16), t_shape] + cast_shapes,
        compiler_params=_params(),
        name="qkv_proj",
    )(x2d, w_q_t, w_k, w_v_t, *[w for w, _ in cast_weights])
    return outs[0], outs[1], outs[2], outs[3:]


_LOG2E_PIECES = _bf16_pieces(LOG2E)


def _attn_kernel(slopes_ref, lam_ref, qt_ref, k_ref, vt_ref, g_ref, *refs,
                 lambda_init, n_cast):
    cast_src = refs[:n_cast]
    o_ref = refs[n_cast]
    cast_dst = refs[n_cast + 1:2 * n_cast + 1]
    kaug_ref, qs_ref, s_ref = refs[2 * n_cast + 1:]
    _run_casts(cast_src, cast_dst)
    T = ATTN_T
    DQ = 2 * HEAD_DIM_QK
    seq = k_ref.shape[1]
    n_tiles = seq // T
    h = pl.program_id(1)
    slope = slopes_ref[h]

    pos = lax.broadcasted_iota(jnp.int32, (seq, DQ), 0)
    lane_k = lax.broadcasted_iota(jnp.int32, (seq, DQ), 1)
    j_hi = (pos // CHUNK).astype(_F32) * slope
    j_lo = (pos % CHUNK).astype(_F32) * slope
    key_aug = jnp.where(lane_k < 3, j_hi, jnp.where(lane_k < 6, j_lo, 0.0))
    kaug_ref[:, 0:DQ] = k_ref[0]
    kaug_ref[:, DQ:2 * DQ] = key_aug.astype(_BF16)

    row_q = lax.broadcasted_iota(jnp.int32, (DQ, 2 * T), 0)
    query_aug = jnp.zeros((DQ, 2 * T), _F32)
    for idx, piece in enumerate(_LOG2E_PIECES):
        query_aug = jnp.where(row_q == idx, CHUNK * piece, query_aug)
        query_aug = jnp.where(row_q == idx + 3, piece, query_aug)
    query_aug = query_aug.astype(_BF16)

    lp = lam_ref[...]
    lam = (jnp.exp(jnp.sum(lp[0:1] * lp[1:2], axis=-1, keepdims=True))
           - jnp.exp(jnp.sum(lp[2:3] * lp[3:4], axis=-1, keepdims=True))
           + lambda_init)

    key_i = lax.broadcasted_iota(jnp.int32, (T, T), 0)
    qry_i = lax.broadcasted_iota(jnp.int32, (T, T), 1)
    ahead = jnp.maximum(key_i - qry_i, 0).astype(_F32)
    allowed = (key_i // CHUNK) <= (qry_i // CHUNK)
    diag_add = jnp.where(allowed, (-2.0 * LOG2E * slope) * ahead, MASK_VALUE)
    diag_add = jnp.concatenate([diag_add, diag_add], axis=1)

    row_d = lax.broadcasted_iota(jnp.int32, (DQ, T), 0)
    g_scaled = g_ref[...] * (1.0 - lambda_init)

    def stage_queries(t):
        qt = qt_ref[0, :, t * T:(t + 1) * T]
        zero = jnp.zeros_like(qt)
        top = jnp.concatenate([jnp.where(row_d < HEAD_DIM_QK, qt, zero),
                               jnp.where(row_d >= HEAD_DIM_QK, qt, zero)], axis=1)
        qs_ref[t % SCORE_SLOTS] = jnp.concatenate([top, query_aug], axis=0)

    def score_pass(t, j, m):
        rows = slice(j * T, (j + 1) * T)
        s = jnp.dot(kaug_ref[rows, :], qs_ref[t % SCORE_SLOTS],
                    preferred_element_type=_F32)
        if j == t:
            s = s + diag_add
        s_ref[t % SCORE_SLOTS, rows, :] = s
        m_blk = jnp.max(s, axis=0, keepdims=True)
        return m_blk if m is None else jnp.maximum(m, m_blk)

    def value_pass(t, j, m, l, acc):
        rows = slice(j * T, (j + 1) * T)
        p = jnp.exp2(s_ref[t % SCORE_SLOTS, rows, :] - m)
        l_blk = jnp.sum(p, axis=0, keepdims=True)
        acc_blk = jnp.dot(vt_ref[0, :, rows], p.astype(_BF16),
                          preferred_element_type=_F32)
        if l is None:
            return l_blk, acc_blk
        return l + l_blk, acc + acc_blk

    def finish(t, l, acc):
        att = acc / l
        o = (att[:, 0:T] - lam * att[:, T:2 * T]).T
        ms = jnp.mean(o * o, axis=-1, keepdims=True)
        o = o * lax.rsqrt(ms + LN_EPS) * g_scaled
        o_ref[0, t * T:(t + 1) * T, :] = o.astype(_BF16)

    order = list(range(n_tiles - 1, -1, -1))
    col_max = {}
    for i in range(n_tiles + 1):
        if i < n_tiles:
            t = order[i]
            stage_queries(t)
            m = None
            for j in range(t + 1):
                m = score_pass(t, j, m)
            col_max[t] = m
        if i > 0:
            t = order[i - 1]
            l, acc = None, None
            for j in range(t + 1):
                l, acc = value_pass(t, j, col_max[t], l, acc)
            finish(t, l, acc)


def _attention(qt, k, vt, slopes, lam_params, g_sub, lambda_init, cast_weights):
    b, s, _ = k.shape
    head_spec = pl.BlockSpec((1, s, HEAD_DIM_V), lambda bi, hi: (bi, 0, hi))
    head_t_spec = pl.BlockSpec((1, HEAD_DIM_V, s), lambda bi, hi: (bi, hi, 0))
    cast_in, cast_out, cast_shapes = _cast_plan(
        cast_weights, b * N_HEADS, lambda bi, hi: bi * N_HEADS + hi)
    outs = pl.pallas_call(
        functools.partial(_attn_kernel, lambda_init=lambda_init, n_cast=len(cast_weights)),
        grid=(b, N_HEADS),
        in_specs=[
            pl.BlockSpec(memory_space=pltpu.SMEM),
            pl.BlockSpec((4, HEAD_DIM_QK), lambda bi, hi: (0, 0)),
            head_t_spec, head_spec, head_t_spec,
            pl.BlockSpec((1, HEAD_DIM_V), lambda bi, hi: (0, 0)),
        ] + cast_in,
        out_specs=[head_spec] + cast_out,
        out_shape=[jax.ShapeDtypeStruct((b, s, N_HEADS * HEAD_DIM_V), _BF16)] + cast_shapes,
        scratch_shapes=[
            pltpu.VMEM((s, 4 * HEAD_DIM_QK), _BF16),
            pltpu.VMEM((SCORE_SLOTS, 4 * HEAD_DIM_QK, 2 * ATTN_T), _BF16),
            pltpu.VMEM((SCORE_SLOTS, s, 2 * ATTN_T), _F32),
        ],
        compiler_params=_params(2),
        name="diff_attention",
    )(slopes, lam_params, qt, k, vt, g_sub, *[w for w, _ in cast_weights])
    return outs[0], outs[1:]


def _ffn_kernel(*refs, with_out_proj):
    if with_out_proj:
        o_ref, wo_ref, mix_g_ref, mix_b_ref = refs[:4]
        refs = refs[4:]
    x_ref, wg_ref, wu_ref, wd_ref, g_ref, b_ref, y_ref = refs
    rows_per = FFN_TM // FFN_SPLIT
    groups = [slice(i * rows_per, (i + 1) * rows_per) for i in range(FFN_SPLIT)]
    xs, acts = [], []
    for rows in groups:
        x = x_ref[rows, :]
        if with_out_proj:
            mix = _dot(o_ref[rows, :], wo_ref[0])
            x = _layernorm(DEEPNORM_ALPHA * x + mix, mix_g_ref[...], mix_b_ref[...])
        xs.append(x)
    for x in xs:
        xb = x.astype(_BF16)
        gate = _dot(xb, wg_ref[0])
        up = _dot(xb, wu_ref[0])
        acts.append((gate * jax.nn.sigmoid(gate) * up).astype(_BF16))
    for rows, x, act in zip(groups, xs, acts):
        y = _dot(act, wd_ref[0])
        y_ref[rows, :] = _layernorm(DEEPNORM_ALPHA * x + y, g_ref[...], b_ref[...])


def _ffn(x2d, w_gate, w_up, w_down, g, b, out_proj=None):
    m = x2d.shape[0]
    row_spec = pl.BlockSpec((FFN_TM, D_MODEL), lambda i: (i, 0))
    vec_spec = pl.BlockSpec((1, D_MODEL), lambda i: (0, 0))
    in_specs = [row_spec, _resident((1, D_MODEL, D_FF)), _resident((1, D_MODEL, D_FF)),
                _resident((1, D_FF, D_MODEL)), vec_spec, vec_spec]
    args = [x2d, w_gate, w_up, w_down, g, b]
    if out_proj is not None:
        in_specs = [row_spec, _resident((1, D_MODEL, D_MODEL)), vec_spec, vec_spec] + in_specs
        args = list(out_proj) + args
    return pl.pallas_call(
        functools.partial(_ffn_kernel, with_out_proj=out_proj is not None),
        grid=(m // FFN_TM,),
        in_specs=in_specs,
        out_specs=row_spec,
        out_shape=jax.ShapeDtypeStruct((m, D_MODEL), _F32),
        compiler_params=_params(),
        name="ffn_ln",
    )(*args)


def _gmlp_kernel(x_ref, xnext_ref, win_ref, bin_ref, lng_ref, lnb_ref, ws_ref, bs_ref,
                 wout_ref, g_ref, b_ref, y_ref, za_ref, zb_ref, gated_a_ref, gated_b_ref):
    tm = GMLP_TM
    row = lax.broadcasted_iota(jnp.int32, (GMLP_BLOCK, GMLP_BLOCK), 0)
    col = lax.broadcasted_iota(jnp.int32, (GMLP_BLOCK, GMLP_BLOCK), 1)
    tri = (row >= col).astype(_F32)
    bs = bs_ref[...]

    def stage1(x, z_ref):
        z = _dot(x.astype(_BF16), win_ref[...]) + bin_ref[...]
        z_ref[...] = 0.5 * z * (1.0 + lax.erf(z * (2.0 ** -0.5)))

    def stage2(x, z_ref, gated_ref, out_rows):
        v = _layernorm(z_ref[:, GMLP_HALF:], lng_ref[...], lnb_ref[...]).astype(_BF16)
        for grp in range(GMLP_GROUPS):
            w_mix = (ws_ref[grp] * tri).astype(_BF16)
            cols = slice(grp * GMLP_GROUP_DIM, (grp + 1) * GMLP_GROUP_DIM)
            for blk in range(tm // GMLP_BLOCK):
                rows = slice(blk * GMLP_BLOCK, (blk + 1) * GMLP_BLOCK)
                mixed = _dot(w_mix, v[rows, cols]) + bs[:, grp:grp + 1]
                gated_ref[rows, cols] = (z_ref[rows, cols] * mixed).astype(_BF16)
        y = _dot(gated_ref[...], wout_ref[...])
        y_ref[out_rows, :] = _layernorm(DEEPNORM_ALPHA * x + y, g_ref[...], b_ref[...])

    @pl.when(pl.program_id(0) == 0)
    def _():
        stage1(x_ref[0:tm, :], za_ref)

    stage1(x_ref[tm:2 * tm, :], zb_ref)
    stage2(x_ref[0:tm, :], za_ref, gated_a_ref, slice(0, tm))
    stage1(xnext_ref[...], za_ref)
    stage2(x_ref[tm:2 * tm, :], zb_ref, gated_b_ref, slice(tm, 2 * tm))


def _gmlp(x2d, w_in, b_in, ln_g, ln_b, w_s, b_s_t, w_out, g, b):
    m = x2d.shape[0]
    steps = m // (2 * GMLP_TM)
    pair_spec = pl.BlockSpec((2 * GMLP_TM, D_MODEL), lambda i: (i, 0))
    next_spec = pl.BlockSpec((GMLP_TM, D_MODEL),
                             lambda i: (jnp.minimum(2 * i + 2, 2 * steps - 1), 0))
    vec_spec = pl.BlockSpec((1, D_MODEL), lambda i: (0, 0))
    half_spec = pl.BlockSpec((1, GMLP_HALF), lambda i: (0, 0))
    return pl.pallas_call(
        _gmlp_kernel,
        grid=(steps,),
        in_specs=[pair_spec, next_spec,
                  _resident((D_MODEL, 2 * GMLP_HALF)),
                  pl.BlockSpec((1, 2 * GMLP_HALF), lambda i: (0, 0)),
                  half_spec, half_spec,
                  _resident((GMLP_GROUPS, GMLP_BLOCK, GMLP_BLOCK)),
                  pl.BlockSpec((GMLP_BLOCK, GMLP_GROUPS), lambda i: (0, 0)),
                  _resident((GMLP_HALF, D_MODEL)),
                  vec_spec, vec_spec],
        out_specs=pair_spec,
        out_shape=jax.ShapeDtypeStruct((m, D_MODEL), _F32),
        scratch_shapes=[pltpu.VMEM((GMLP_TM, 2 * GMLP_HALF), _F32),
                        pltpu.VMEM((GMLP_TM, 2 * GMLP_HALF), _F32),
                        pltpu.VMEM((GMLP_TM, GMLP_HALF), _BF16),
                        pltpu.VMEM((GMLP_TM, GMLP_HALF), _BF16)],
        compiler_params=_params(),
        name="gmlp_ln",
    )(x2d, x2d, w_in, b_in, ln_g, ln_b, w_s, b_s_t, w_out, g, b)


def kernel(x, attn_w_qkv, attn_lambda_q1, attn_lambda_k1, attn_lambda_q2, attn_lambda_k2,
           attn_subln_g, attn_w_o, gmlp_w_in, gmlp_b_in, gmlp_ln_g, gmlp_ln_b, gmlp_w_s,
           gmlp_b_s, gmlp_w_out, ln_mix_g, ln_mix_b, ffn_w_gate, ffn_w_up, ffn_w_down,
           ln_ffn_g, ln_ffn_b):
    bsz, seq, d = x.shape
    assert d == D_MODEL and seq % PROJ_TM == 0 and seq % ATTN_T == 0
    m = bsz * seq
    x2d = x.reshape(m, d)
    slopes = 2.0 ** (-8.0 * jnp.arange(1, N_HEADS + 1, dtype=_F32) / N_HEADS)

    def vec(a):
        return a.reshape(1, -1)

    w_qkv = attn_w_qkv[0]
    w_down_view = ffn_w_down.reshape(DEPTH, D_MODEL, D_FF)
    qt, k, vt, (w_o, w_gate0, w_up0, w_down0) = _qkv_proj(
        x2d, w_qkv[:, :D_MODEL].T.astype(_BF16), w_qkv[:, D_MODEL:2 * D_MODEL].astype(_BF16),
        w_qkv[:, 2 * D_MODEL:].T.astype(_BF16), bsz, seq,
        [(attn_w_o, 0), (ffn_w_gate, 0), (ffn_w_up, 0), (w_down_view, 0)])
    lam_params = jnp.stack([attn_lambda_q1[0], attn_lambda_k1[0],
                            attn_lambda_q2[0], attn_lambda_k2[0]])
    o, (w_gate1, w_up1, w_down1, w_in, w_out) = _attention(
        qt, k.reshape(bsz, seq, d), vt, slopes, lam_params, vec(attn_subln_g[0]),
        _lambda_init(0),
        [(ffn_w_gate, 1), (ffn_w_up, 1), (w_down_view, 1), (gmlp_w_in, 0), (gmlp_w_out, 0)])
    x2d = _ffn(x2d, w_gate0, w_up0, w_down0.reshape(1, D_FF, D_MODEL),
               vec(ln_ffn_g[0]), vec(ln_ffn_b[0]),
               out_proj=(o.reshape(m, d), w_o, vec(ln_mix_g[0]), vec(ln_mix_b[0])))

    x2d = _gmlp(x2d, w_in[0], vec(gmlp_b_in[0]), vec(gmlp_ln_g[0]),
                vec(gmlp_ln_b[0]), gmlp_w_s[0], gmlp_b_s[0].T, w_out[0],
                vec(ln_mix_g[1]), vec(ln_mix_b[1]))
    x2d = _ffn(x2d, w_gate1, w_up1, w_down1.reshape(1, D_FF, D_MODEL),
               vec(ln_ffn_g[1]), vec(ln_ffn_b[1]))
    return x2d.reshape(bsz, seq, d)
```

```python
import functools
import math
import struct

import jax
import jax.numpy as jnp
from jax import lax
from jax.experimental import pallas as pl
from jax.experimental.pallas import tpu as pltpu

D_MODEL = 1024
DEPTH = 2
CHUNK = 64
N_HEADS = 8
HEAD_DIM_QK = 64
HEAD_DIM_V = 128
GMLP_BLOCK = 128
GMLP_HALF = 2 * D_MODEL
GMLP_GROUPS = 8
GMLP_GROUP_DIM = GMLP_HALF // GMLP_GROUPS
D_FF = 2816
DEEPNORM_ALPHA = (2 * DEPTH) ** 0.25
LN_EPS = 1e-5
MASK_VALUE = -1e30
QK_SCALE = HEAD_DIM_QK ** -0.5
LOG2E = math.log2(math.e)

V7X_VMEM_LIMIT_BYTES = 56 * 1024 * 1024
BF16_SUBLANES = 16

PROJ_TM = 512
FFN_TM = 512
FFN_SPLIT = 2
GMLP_TM = 256
ATTN_T = 256
SCORE_SLOTS = 2

_BF16 = jnp.bfloat16
_F32 = jnp.float32


def _bf16_round(x):
    bits = struct.unpack("<I", struct.pack("<f", x))[0]
    bits = (bits + 0x7FFF + ((bits >> 16) & 1)) & 0xFFFF0000
    return struct.unpack("<f", struct.pack("<I", bits))[0]


def _bf16_pieces(x, n=3):
    pieces = []
    for _ in range(n):
        p = _bf16_round(x)
        pieces.append(p)
        x -= p
    return pieces


def _lambda_init(layer_idx):
    return 0.8 - 0.6 * math.exp(-0.3 * layer_idx)


def _layernorm(y, g, b):
    mu = jnp.mean(y, axis=-1, keepdims=True)
    d = y - mu
    var = jnp.mean(d * d, axis=-1, keepdims=True)
    return d * lax.rsqrt(var + LN_EPS) * g + b


def _dot(a, b):
    return jnp.dot(a, b, preferred_element_type=_F32)


def _dot_nt(a, b):
    return lax.dot_general(a, b, (((1,), (1,)), ((), ())), preferred_element_type=_F32)


def _resident(shape):
    return pl.BlockSpec(shape, lambda *_: (0,) * len(shape), pipeline_mode=pl.Buffered(1))


def _params(n_parallel_axes=1, flags=None):
    return pltpu.CompilerParams(
        dimension_semantics=("arbitrary",) * n_parallel_axes,
        vmem_limit_bytes=V7X_VMEM_LIMIT_BYTES,
        flags=flags,
    )


def _cast_plan(weights, steps, step_of):
    in_specs, out_specs, out_shapes = [], [], []
    for w, layer in weights:
        _, rows, cols = w.shape
        slices = steps
        while rows % (slices * BF16_SUBLANES):
            assert slices % 2 == 0, (w.shape, steps)
            slices //= 2
        hold = steps // slices
        block = (1, rows // slices, cols)
        in_specs.append(pl.BlockSpec(
            block, lambda *g, layer=layer, hold=hold: (layer, step_of(*g) // hold, 0)))
        out_specs.append(pl.BlockSpec(block, lambda *g, hold=hold: (0, step_of(*g) // hold, 0)))
        out_shapes.append(jax.ShapeDtypeStruct((1, rows, cols), _BF16))
    return in_specs, out_specs, out_shapes


def _run_casts(src_refs, dst_refs):
    for src, dst in zip(src_refs, dst_refs):
        dst[...] = src[...].astype(_BF16)


def _qkv_kernel(x_ref, wqt_ref, wk_ref, wvt_ref, *refs, n_cast):
    cast_src = refs[:n_cast]
    qt_ref, k_ref, vt_ref = refs[n_cast:n_cast + 3]
    cast_dst = refs[n_cast + 3:]
    xb = x_ref[...].astype(_BF16)
    qt_ref[0] = (_dot_nt(wqt_ref[...], xb) * (QK_SCALE * LOG2E)).astype(_BF16)
    k_ref[...] = _dot(xb, wk_ref[...]).astype(_BF16)
    vt_ref[0] = _dot_nt(wvt_ref[...], xb).astype(_BF16)
    _run_casts(cast_src, cast_dst)


def _qkv_proj(x2d, w_q_t, w_k, w_v_t, bsz, seq, cast_weights):
    m = x2d.shape[0]
    steps = m // PROJ_TM
    tiles_per_seq = seq // PROJ_TM
    row_spec = pl.BlockSpec((PROJ_TM, D_MODEL), lambda i: (i, 0))
    t_spec = pl.BlockSpec((1, D_MODEL, PROJ_TM),
                          lambda i: (i // tiles_per_seq, 0, i % tiles_per_seq))
    w_spec = _resident((D_MODEL, D_MODEL))
    t_shape = jax.ShapeDtypeStruct((bsz, D_MODEL, seq), _BF16)
    cast_in, cast_out, cast_shapes = _cast_plan(cast_weights, steps, lambda i: i)

    outs = pl.pallas_call(
        functools.partial(_qkv_kernel, n_cast=len(cast_weights)),
        grid=(steps,),
        in_specs=[row_spec, w_spec, w_spec, w_spec] + cast_in,
        out_specs=[t_spec, row_spec, t_spec] + cast_out,
        out_shape=[t_shape, jax.ShapeDtypeStruct((m, D_MODEL), _BF16), t_shape] + cast_shapes,
        compiler_params=_params(),
        name="qkv_proj",
    )(x2d, w_q_t, w_k, w_v_t, *[w for w, _ in cast_weights])
    return outs[0], outs[1], outs[2], outs[3:]


_LOG2E_PIECES = _bf16_pieces(LOG2E)


def _attn_kernel(slopes_ref, lam_ref, qt_ref, k_ref, vt_ref, g_ref, o_ref,
                 kaug_ref, qs_ref, s_ref, *, lambda_init):
    T = ATTN_T
    DQ = 2 * HEAD_DIM_QK
    seq = k_ref.shape[1]
    n_tiles = seq // T
    h = pl.program_id(1)
    slope = slopes_ref[h]

    pos = lax.broadcasted_iota(jnp.int32, (seq, DQ), 0)
    lane_k = lax.broadcasted_iota(jnp.int32, (seq, DQ), 1)
    j_hi = (pos // CHUNK).astype(_F32) * slope
    j_lo = (pos % CHUNK).astype(_F32) * slope
    key_aug = jnp.where(lane_k < 3, j_hi, jnp.where(lane_k < 6, j_lo, 0.0))
    kaug_ref[:, 0:DQ] = k_ref[0]
    kaug_ref[:, DQ:2 * DQ] = key_aug.astype(_BF16)

    row_q = lax.broadcasted_iota(jnp.int32, (DQ, 2 * T), 0)
    query_aug = jnp.zeros((DQ, 2 * T), _F32)
    for idx, piece in enumerate(_LOG2E_PIECES):
        query_aug = jnp.where(row_q == idx, CHUNK * piece, query_aug)
        query_aug = jnp.where(row_q == idx + 3, piece, query_aug)
    query_aug = query_aug.astype(_BF16)

    lp = lam_ref[...]
    lam = (jnp.exp(jnp.sum(lp[0:1] * lp[1:2], axis=-1, keepdims=True))
           - jnp.exp(jnp.sum(lp[2:3] * lp[3:4], axis=-1, keepdims=True))
           + lambda_init)

    key_i = lax.broadcasted_iota(jnp.int32, (T, T), 0)
    qry_i = lax.broadcasted_iota(jnp.int32, (T, T), 1)
    ahead = jnp.maximum(key_i - qry_i, 0).astype(_F32)
    allowed = (key_i // CHUNK) <= (qry_i // CHUNK)
    diag_add = jnp.where(allowed, (-2.0 * LOG2E * slope) * ahead, MASK_VALUE)
    diag_add = jnp.concatenate([diag_add, diag_add], axis=1)

    row_d = lax.broadcasted_iota(jnp.int32, (DQ, T), 0)
    g_scaled = g_ref[...] * (1.0 - lambda_init)

    def stage_queries(t):
        qt = qt_ref[0, :, t * T:(t + 1) * T]
        zero = jnp.zeros_like(qt)
        top = jnp.concatenate([jnp.where(row_d < HEAD_DIM_QK, qt, zero),
                               jnp.where(row_d >= HEAD_DIM_QK, qt, zero)], axis=1)
        qs_ref[t % SCORE_SLOTS] = jnp.concatenate([top, query_aug], axis=0)

    def score_pass(t, j, m):
        rows = slice(j * T, (j + 1) * T)
        s = jnp.dot(kaug_ref[rows, :], qs_ref[t % SCORE_SLOTS],
                    preferred_element_type=_F32)
        if j == t:
            s = s + diag_add
        s_ref[t % SCORE_SLOTS, rows, :] = s
        m_blk = jnp.max(s, axis=0, keepdims=True)
        return m_blk if m is None else jnp.maximum(m, m_blk)

    def value_pass(t, j, m, l, acc):
        rows = slice(j * T, (j + 1) * T)
        p = jnp.exp2(s_ref[t % SCORE_SLOTS, rows, :] - m)
        l_blk = jnp.sum(p, axis=0, keepdims=True)
        acc_blk = jnp.dot(vt_ref[0, :, rows], p.astype(_BF16),
                          preferred_element_type=_F32)
        if l is None:
            return l_blk, acc_blk
        return l + l_blk, acc + acc_blk

    def finish(t, l, acc):
        att = acc / l
        o = (att[:, 0:T] - lam * att[:, T:2 * T]).T
        ms = jnp.mean(o * o, axis=-1, keepdims=True)
        o = o * lax.rsqrt(ms + LN_EPS) * g_scaled
        o_ref[0, t * T:(t + 1) * T, :] = o.astype(_BF16)

    order = list(range(n_tiles - 1, -1, -1))
    col_max = {}
    for i in range(n_tiles + 1):
        if i < n_tiles:
            t = order[i]
            stage_queries(t)
            m = None
            for j in range(t + 1):
                m = score_pass(t, j, m)
            col_max[t] = m
        if i > 0:
            t = order[i - 1]
            l, acc = None, None
            for j in range(t + 1):
                l, acc = value_pass(t, j, col_max[t], l, acc)
            finish(t, l, acc)


def _attention(qt, k, vt, slopes, lam_params, g_sub, lambda_init):
    b, s, _ = k.shape
    head_spec = pl.BlockSpec((1, s, HEAD_DIM_V), lambda bi, hi: (bi, 0, hi))
    head_t_spec = pl.BlockSpec((1, HEAD_DIM_V, s), lambda bi, hi: (bi, hi, 0))
    return pl.pallas_call(
        functools.partial(_attn_kernel, lambda_init=lambda_init),
        grid=(b, N_HEADS),
        in_specs=[
            pl.BlockSpec(memory_space=pltpu.SMEM),
            pl.BlockSpec((4, HEAD_DIM_QK), lambda bi, hi: (0, 0)),
            head_t_spec, head_spec, head_t_spec,
            pl.BlockSpec((1, HEAD_DIM_V), lambda bi, hi: (0, 0)),
        ],
        out_specs=head_spec,
        out_shape=jax.ShapeDtypeStruct((b, s, N_HEADS * HEAD_DIM_V), _BF16),
        scratch_shapes=[
            pltpu.VMEM((s, 4 * HEAD_DIM_QK), _BF16),
            pltpu.VMEM((SCORE_SLOTS, 4 * HEAD_DIM_QK, 2 * ATTN_T), _BF16),
            pltpu.VMEM((SCORE_SLOTS, s, 2 * ATTN_T), _F32),
        ],
        compiler_params=_params(2),
        name="diff_attention",
    )(slopes, lam_params, qt, k, vt, g_sub)


def _ffn_kernel(*refs, with_out_proj):
    if with_out_proj:
        o_ref, wo_ref, mix_g_ref, mix_b_ref = refs[:4]
        refs = refs[4:]
    x_ref, wg_ref, wu_ref, wd_ref, g_ref, b_ref, y_ref = refs
    rows_per = FFN_TM // FFN_SPLIT
    groups = [slice(i * rows_per, (i + 1) * rows_per) for i in range(FFN_SPLIT)]
    xs, acts = [], []
    for rows in groups:
        x = x_ref[rows, :]
        if with_out_proj:
            mix = _dot(o_ref[rows, :], wo_ref[0])
            x = _layernorm(DEEPNORM_ALPHA * x + mix, mix_g_ref[...], mix_b_ref[...])
        xs.append(x)
    for x in xs:
        xb = x.astype(_BF16)
        gate = _dot(xb, wg_ref[0])
        up = _dot(xb, wu_ref[0])
        acts.append((gate * jax.nn.sigmoid(gate) * up).astype(_BF16))
    for rows, x, act in zip(groups, xs, acts):
        y = _dot(act, wd_ref[0])
        y_ref[rows, :] = _layernorm(DEEPNORM_ALPHA * x + y, g_ref[...], b_ref[...])


def _ffn(x2d, w_gate, w_up, w_down, g, b, out_proj=None):
    m = x2d.shape[0]
    row_spec = pl.BlockSpec((FFN_TM, D_MODEL), lambda i: (i, 0))
    vec_spec = pl.BlockSpec((1, D_MODEL), lambda i: (0, 0))
    in_specs = [row_spec, _resident((1, D_MODEL, D_FF)), _resident((1, D_MODEL, D_FF)),
                _resident((1, D_FF, D_MODEL)), vec_spec, vec_spec]
    args = [x2d, w_gate, w_up, w_down, g, b]
    if out_proj is not None:
        in_specs = [row_spec, _resident((1, D_MODEL, D_MODEL)), vec_spec, vec_spec] + in_specs
        args = list(out_proj) + args
    return pl.pallas_call(
        functools.partial(_ffn_kernel, with_out_proj=out_proj is not None),
        grid=(m // FFN_TM,),
        in_specs=in_specs,
        out_specs=row_spec,
        out_shape=jax.ShapeDtypeStruct((m, D_MODEL), _F32),
        compiler_params=_params(),
        name="ffn_ln",
    )(*args)


def _gmlp_kernel(x_ref, xnext_ref, win_ref, bin_ref, lng_ref, lnb_ref, ws_ref, bs_ref,
                 wout_ref, g_ref, b_ref, y_ref, za_ref, zb_ref, gated_a_ref, gated_b_ref):
    tm = GMLP_TM
    row = lax.broadcasted_iota(jnp.int32, (GMLP_BLOCK, GMLP_BLOCK), 0)
    col = lax.broadcasted_iota(jnp.int32, (GMLP_BLOCK, GMLP_BLOCK), 1)
    tri = (row >= col).astype(_F32)
    bs = bs_ref[...]

    def stage1(x, z_ref):
        z = _dot(x.astype(_BF16), win_ref[...]) + bin_ref[...]
        z_ref[...] = 0.5 * z * (1.0 + lax.erf(z * (2.0 ** -0.5)))

    def stage2(x, z_ref, gated_ref, out_rows):
        v = _layernorm(z_ref[:, GMLP_HALF:], lng_ref[...], lnb_ref[...]).astype(_BF16)
        for grp in range(GMLP_GROUPS):
            w_mix = (ws_ref[grp] * tri).astype(_BF16)
            cols = slice(grp * GMLP_GROUP_DIM, (grp + 1) * GMLP_GROUP_DIM)
            for blk in range(tm // GMLP_BLOCK):
                rows = slice(blk * GMLP_BLOCK, (blk + 1) * GMLP_BLOCK)
                mixed = _dot(w_mix, v[rows, cols]) + bs[:, grp:grp + 1]
                gated_ref[rows, cols] = (z_ref[rows, cols] * mixed).astype(_BF16)
        y = _dot(gated_ref[...], wout_ref[...])
        y_ref[out_rows, :] = _layernorm(DEEPNORM_ALPHA * x + y, g_ref[...], b_ref[...])

    @pl.when(pl.program_id(0) == 0)
    def _():
        stage1(x_ref[0:tm, :], za_ref)

    stage1(x_ref[tm:2 * tm, :], zb_ref)
    stage2(x_ref[0:tm, :], za_ref, gated_a_ref, slice(0, tm))
    stage1(xnext_ref[...], za_ref)
    stage2(x_ref[tm:2 * tm, :], zb_ref, gated_b_ref, slice(tm, 2 * tm))


def _gmlp(x2d, w_in, b_in, ln_g, ln_b, w_s, b_s_t, w_out, g, b):
    m = x2d.shape[0]
    steps = m // (2 * GMLP_TM)
    pair_spec = pl.BlockSpec((2 * GMLP_TM, D_MODEL), lambda i: (i, 0))
    next_spec = pl.BlockSpec((GMLP_TM, D_MODEL),
                             lambda i: (jnp.minimum(2 * i + 2, 2 * steps - 1), 0))
    vec_spec = pl.BlockSpec((1, D_MODEL), lambda i: (0, 0))
    half_spec = pl.BlockSpec((1, GMLP_HALF), lambda i: (0, 0))
    return pl.pallas_call(
        _gmlp_kernel,
        grid=(steps,),
        in_specs=[pair_spec, next_spec,
                  _resident((D_MODEL, 2 * GMLP_HALF)),
                  pl.BlockSpec((1, 2 * GMLP_HALF), lambda i: (0, 0)),
                  half_spec, half_spec,
                  _resident((GMLP_GROUPS, GMLP_BLOCK, GMLP_BLOCK)),
                  pl.BlockSpec((GMLP_BLOCK, GMLP_GROUPS), lambda i: (0, 0)),
                  _resident((GMLP_HALF, D_MODEL)),
                  vec_spec, vec_spec],
        out_specs=pair_spec,
        out_shape=jax.ShapeDtypeStruct((m, D_MODEL), _F32),
        scratch_shapes=[pltpu.VMEM((GMLP_TM, 2 * GMLP_HALF), _F32),
                        pltpu.VMEM((GMLP_TM, 2 * GMLP_HALF), _F32),
                        pltpu.VMEM((GMLP_TM, GMLP_HALF), _BF16),
                        pltpu.VMEM((GMLP_TM, GMLP_HALF), _BF16)],
        compiler_params=_params(),
        name="gmlp_ln",
    )(x2d, x2d, w_in, b_in, ln_g, ln_b, w_s, b_s_t, w_out, g, b)


def kernel(x, attn_w_qkv, attn_lambda_q1, attn_lambda_k1, attn_lambda_q2, attn_lambda_k2,
           attn_subln_g, attn_w_o, gmlp_w_in, gmlp_b_in, gmlp_ln_g, gmlp_ln_b, gmlp_w_s,
           gmlp_b_s, gmlp_w_out, ln_mix_g, ln_mix_b, ffn_w_gate, ffn_w_up, ffn_w_down,
           ln_ffn_g, ln_ffn_b):
    bsz, seq, d = x.shape
    assert d == D_MODEL and seq % PROJ_TM == 0 and seq % ATTN_T == 0
    m = bsz * seq
    x2d = x.reshape(m, d)
    slopes = 2.0 ** (-8.0 * jnp.arange(1, N_HEADS + 1, dtype=_F32) / N_HEADS)

    def vec(a):
        return a.reshape(1, -1)

    w_qkv = attn_w_qkv[0]
    qt, k, vt, (w_o, w_gate0, w_up0, w_down0, w_gate1, w_up1, w_down1, w_in, w_out) = _qkv_proj(
        x2d, w_qkv[:, :D_MODEL].T.astype(_BF16), w_qkv[:, D_MODEL:2 * D_MODEL].astype(_BF16),
        w_qkv[:, 2 * D_MODEL:].T.astype(_BF16), bsz, seq,
        [(attn_w_o, 0), (ffn_w_gate, 0), (ffn_w_up, 0), (ffn_w_down, 0),
         (ffn_w_gate, 1), (ffn_w_up, 1), (ffn_w_down, 1), (gmlp_w_in, 0), (gmlp_w_out, 0)])
    lam_params = jnp.stack([attn_lambda_q1[0], attn_lambda_k1[0],
                            attn_lambda_q2[0], attn_lambda_k2[0]])
    o = _attention(qt, k.reshape(bsz, seq, d), vt, slopes, lam_params,
                   vec(attn_subln_g[0]), _lambda_init(0))
    x2d = _ffn(x2d, w_gate0, w_up0, w_down0, vec(ln_ffn_g[0]), vec(ln_ffn_b[0]),
               out_proj=(o.reshape(m, d), w_o, vec(ln_mix_g[0]), vec(ln_mix_b[0])))

    x2d = _gmlp(x2d, w_in[0], vec(gmlp_b_in[0]), vec(gmlp_ln_g[0]),
                vec(gmlp_ln_b[0]), gmlp_w_s[0], gmlp_b_s[0].T, w_out[0],
                vec(ln_mix_g[1]), vec(ln_mix_b[1]))
    x2d = _ffn(x2d, w_gate1, w_up1, w_down1, vec(ln_ffn_g[1]), vec(ln_ffn_b[1]))
    return x2d.reshape(bsz, seq, d)
```

```python
import functools
import math
import struct

import jax
import jax.numpy as jnp
from jax import lax
from jax.experimental import pallas as pl
from jax.experimental.pallas import tpu as pltpu

D_MODEL = 1024
DEPTH = 2
CHUNK = 64
N_HEADS = 8
HEAD_DIM_QK = 64
HEAD_DIM_V = 128
GMLP_BLOCK = 128
GMLP_HALF = 2 * D_MODEL
GMLP_GROUPS = 8
GMLP_GROUP_DIM = GMLP_HALF // GMLP_GROUPS
D_FF = 2816
DEEPNORM_ALPHA = (2 * DEPTH) ** 0.25
LN_EPS = 1e-5
MASK_VALUE = -1e30
QK_SCALE = HEAD_DIM_QK ** -0.5
LOG2E = math.log2(math.e)

V7X_VMEM_LIMIT_BYTES = 56 * 1024 * 1024
BF16_SUBLANES = 16

PROJ_TM = 512
FFN_TM = 512
FFN_SPLIT = 2
GMLP_TM = 256
ATTN_T = 256
KEY_BLOCK = 512
ATTN_HEADS_PER_STEP = 2
SCORE_SLOTS = 2

_BF16 = jnp.bfloat16
_F32 = jnp.float32


def _bf16_round(x):
    bits = struct.unpack("<I", struct.pack("<f", x))[0]
    bits = (bits + 0x7FFF + ((bits >> 16) & 1)) & 0xFFFF0000
    return struct.unpack("<f", struct.pack("<I", bits))[0]


def _bf16_pieces(x, n=3):
    pieces = []
    for _ in range(n):
        p = _bf16_round(x)
        pieces.append(p)
        x -= p
    return pieces


def _lambda_init(layer_idx):
    return 0.8 - 0.6 * math.exp(-0.3 * layer_idx)


def _layernorm(y, g, b):
    mu = jnp.mean(y, axis=-1, keepdims=True)
    d = y - mu
    var = jnp.mean(d * d, axis=-1, keepdims=True)
    return d * lax.rsqrt(var + LN_EPS) * g + b


def _dot(a, b):
    return jnp.dot(a, b, preferred_element_type=_F32)


def _dot_nt(a, b):
    return lax.dot_general(a, b, (((1,), (1,)), ((), ())), preferred_element_type=_F32)


def _resident(shape):
    return pl.BlockSpec(shape, lambda *_: (0,) * len(shape), pipeline_mode=pl.Buffered(1))


def _params(n_parallel_axes=1, flags=None):
    return pltpu.CompilerParams(
        dimension_semantics=("arbitrary",) * n_parallel_axes,
        vmem_limit_bytes=V7X_VMEM_LIMIT_BYTES,
        flags=flags,
    )


def _cast_plan(weights, steps, step_of):
    in_specs, out_specs, out_shapes = [], [], []
    for w, layer in weights:
        _, rows, cols = w.shape
        slices = steps
        while rows % (slices * BF16_SUBLANES):
            assert slices % 2 == 0, (w.shape, steps)
            slices //= 2
        hold = steps // slices
        block = (1, rows // slices, cols)
        in_specs.append(pl.BlockSpec(
            block, lambda *g, layer=layer, hold=hold: (layer, step_of(*g) // hold, 0)))
        out_specs.append(pl.BlockSpec(block, lambda *g, hold=hold: (0, step_of(*g) // hold, 0)))
        out_shapes.append(jax.ShapeDtypeStruct((1, rows, cols), _BF16))
    return in_specs, out_specs, out_shapes


def _run_casts(src_refs, dst_refs):
    for src, dst in zip(src_refs, dst_refs):
        dst[...] = src[...].astype(_BF16)


def _qkv_kernel(x_ref, wqt_ref, wk_ref, wvt_ref, *refs, n_cast):
    cast_src = refs[:n_cast]
    qt_ref, k_ref, vt_ref = refs[n_cast:n_cast + 3]
    cast_dst = refs[n_cast + 3:]
    xb = x_ref[...].astype(_BF16)
    qt_ref[0] = (_dot_nt(wqt_ref[...], xb) * (QK_SCALE * LOG2E)).astype(_BF16)
    k_ref[...] = _dot(xb, wk_ref[...]).astype(_BF16)
    vt_ref[0] = _dot_nt(wvt_ref[...], xb).astype(_BF16)
    _run_casts(cast_src, cast_dst)


def _qkv_proj(x2d, w_q_t, w_k, w_v_t, bsz, seq, cast_weights):
    m = x2d.shape[0]
    steps = m // PROJ_TM
    tiles_per_seq = seq // PROJ_TM
    row_spec = pl.BlockSpec((PROJ_TM, D_MODEL), lambda i: (i, 0))
    t_spec = pl.BlockSpec((1, D_MODEL, PROJ_TM),
                          lambda i: (i // tiles_per_seq, 0, i % tiles_per_seq))
    w_spec = _resident((D_MODEL, D_MODEL))
    t_shape = jax.ShapeDtypeStruct((bsz, D_MODEL, seq), _BF16)
    cast_in, cast_out, cast_shapes = _cast_plan(cast_weights, steps, lambda i: i)

    outs = pl.pallas_call(
        functools.partial(_qkv_kernel, n_cast=len(cast_weights)),
        grid=(steps,),
        in_specs=[row_spec, w_spec, w_spec, w_spec] + cast_in,
        out_specs=[t_spec, row_spec, t_spec] + cast_out,
        out_shape=[t_shape, jax.ShapeDtypeStruct((m, D_MODEL), _BF16), t_shape] + cast_shapes,
        compiler_params=_params(),
        name="qkv_proj",
    )(x2d, w_q_t, w_k, w_v_t, *[w for w, _ in cast_weights])
    return outs[0], outs[1], outs[2], outs[3:]


_LOG2E_PIECES = _bf16_pieces(LOG2E)


def _attn_kernel(slopes_ref, lam_ref, qt_ref, k_ref, vt_ref, g_ref, o_ref,
                 kaug_ref, qs_ref, s_ref, *, lambda_init):
    T = ATTN_T
    DQ = 2 * HEAD_DIM_QK
    DV = HEAD_DIM_V
    G = ATTN_HEADS_PER_STEP
    seq = k_ref.shape[1]
    n_tiles = seq // T
    slopes = [slopes_ref[pl.program_id(1) * G + g] for g in range(G)]

    pos = lax.broadcasted_iota(jnp.int32, (seq, DQ), 0)
    lane_k = lax.broadcasted_iota(jnp.int32, (seq, DQ), 1)
    j_hi = jnp.where(lane_k < 3, (pos // CHUNK).astype(_F32), 0.0)
    j_lo = jnp.where((lane_k >= 3) & (lane_k < 6), (pos % CHUNK).astype(_F32), 0.0)
    for g in range(G):
        kaug_ref[g, :, 0:DQ] = k_ref[0, :, g * DQ:(g + 1) * DQ]
        kaug_ref[g, :, DQ:2 * DQ] = ((j_hi + j_lo) * slopes[g]).astype(_BF16)

    row_q = lax.broadcasted_iota(jnp.int32, (DQ, 2 * T), 0)
    query_aug = jnp.zeros((DQ, 2 * T), _F32)
    for idx, piece in enumerate(_LOG2E_PIECES):
        query_aug = jnp.where(row_q == idx, CHUNK * piece, query_aug)
        query_aug = jnp.where(row_q == idx + 3, piece, query_aug)
    query_aug = query_aug.astype(_BF16)

    lp = lam_ref[...]
    lam = (jnp.exp(jnp.sum(lp[0:1] * lp[1:2], axis=-1, keepdims=True))
           - jnp.exp(jnp.sum(lp[2:3] * lp[3:4], axis=-1, keepdims=True))
           + lambda_init)

    key_i = lax.broadcasted_iota(jnp.int32, (T, T), 0)
    qry_i = lax.broadcasted_iota(jnp.int32, (T, T), 1)
    ahead = jnp.maximum(key_i - qry_i, 0).astype(_F32)
    allowed = (key_i // CHUNK) <= (qry_i // CHUNK)

    def diag_add(g):
        d = jnp.where(allowed, (-2.0 * LOG2E * slopes[g]) * ahead, MASK_VALUE)
        return jnp.concatenate([d, d], axis=1)

    row_d = lax.broadcasted_iota(jnp.int32, (DQ, T), 0)
    g_scaled = g_ref[...] * (1.0 - lambda_init)

    def stage_queries(g, t, slot):
        qt = qt_ref[0, g * DQ:(g + 1) * DQ, t * T:(t + 1) * T]
        zero = jnp.zeros_like(qt)
        top = jnp.concatenate([jnp.where(row_d < HEAD_DIM_QK, qt, zero),
                               jnp.where(row_d >= HEAD_DIM_QK, qt, zero)], axis=1)
        qs_ref[slot] = jnp.concatenate([top, query_aug], axis=0)

    def key_blocks(t):
        edges = list(range(0, t * T, KEY_BLOCK)) + [t * T]
        return [(lo, hi, False) for lo, hi in zip(edges[:-1], edges[1:])] + \
               [(t * T, (t + 1) * T, True)]

    def score_pass(g, t, slot, blk, m):
        lo, hi, is_diag = blk
        rows = slice(lo, hi)
        s = jnp.dot(kaug_ref[g, rows, :], qs_ref[slot],
                    preferred_element_type=_F32)
        if is_diag:
            s = s + diag_add(g)
        s_ref[slot, rows, :] = s
        m_blk = jnp.max(s, axis=0, keepdims=True)
        return m_blk if m is None else jnp.maximum(m, m_blk)

    def value_pass(g, slot, blk, m, l, acc):
        rows = slice(blk[0], blk[1])
        p = jnp.exp2(s_ref[slot, rows, :] - m)
        l_blk = jnp.sum(p, axis=0, keepdims=True)
        acc_blk = jnp.dot(vt_ref[0, g * DV:(g + 1) * DV, rows], p.astype(_BF16),
                          preferred_element_type=_F32)
        if l is None:
            return l_blk, acc_blk
        return l + l_blk, acc + acc_blk

    def finish(g, t, l, acc):
        att = acc / l
        o = (att[:, 0:T] - lam * att[:, T:2 * T]).T
        ms = jnp.mean(o * o, axis=-1, keepdims=True)
        o = o * lax.rsqrt(ms + LN_EPS) * g_scaled
        o_ref[0, t * T:(t + 1) * T, g * DV:(g + 1) * DV] = o.astype(_BF16)

    work = []
    for g in range(G):
        tiles = range(n_tiles) if g % 2 == 0 else range(n_tiles - 1, -1, -1)
        work += [(g, t) for t in tiles]
    col_max = {}
    for i in range(len(work) + 1):
        if i < len(work):
            g, t = work[i]
            stage_queries(g, t, i % SCORE_SLOTS)
            m = None
            for blk in key_blocks(t):
                m = score_pass(g, t, i % SCORE_SLOTS, blk, m)
            col_max[i] = m
        if i > 0:
            g, t = work[i - 1]
            l, acc = None, None
            for blk in key_blocks(t):
                l, acc = value_pass(g, (i - 1) % SCORE_SLOTS, blk, col_max[i - 1], l, acc)
            finish(g, t, l, acc)


def _attention(qt, k, vt, slopes, lam_params, g_sub, lambda_init):
    b, s, _ = k.shape
    width = ATTN_HEADS_PER_STEP * HEAD_DIM_V
    head_spec = pl.BlockSpec((1, s, width), lambda bi, hi: (bi, 0, hi))
    head_t_spec = pl.BlockSpec((1, width, s), lambda bi, hi: (bi, hi, 0))
    return pl.pallas_call(
        functools.partial(_attn_kernel, lambda_init=lambda_init),
        grid=(b, N_HEADS // ATTN_HEADS_PER_STEP),
        in_specs=[
            pl.BlockSpec(memory_space=pltpu.SMEM),
            pl.BlockSpec((4, HEAD_DIM_QK), lambda bi, hi: (0, 0)),
            head_t_spec, head_spec, head_t_spec,
            pl.BlockSpec((1, HEAD_DIM_V), lambda bi, hi: (0, 0)),
        ],
        out_specs=head_spec,
        out_shape=jax.ShapeDtypeStruct((b, s, N_HEADS * HEAD_DIM_V), _BF16),
        scratch_shapes=[
            pltpu.VMEM((ATTN_HEADS_PER_STEP, s, 4 * HEAD_DIM_QK), _BF16),
            pltpu.VMEM((SCORE_SLOTS, 4 * HEAD_DIM_QK, 2 * ATTN_T), _BF16),
            pltpu.VMEM((SCORE_SLOTS, s, 2 * ATTN_T), _F32),
        ],
        compiler_params=_params(2),
        name="diff_attention",
    )(slopes, lam_params, qt, k, vt, g_sub)


def _ffn_kernel(*refs, with_out_proj):
    if with_out_proj:
        o_ref, wo_ref, mix_g_ref, mix_b_ref = refs[:4]
        refs = refs[4:]
    x_ref, wg_ref, wu_ref, wd_ref, g_ref, b_ref, y_ref = refs
    rows_per = FFN_TM // FFN_SPLIT
    groups = [slice(i * rows_per, (i + 1) * rows_per) for i in range(FFN_SPLIT)]
    xs, acts = [], []
    for rows in groups:
        x = x_ref[rows, :]
        if with_out_proj:
            mix = _dot(o_ref[rows, :], wo_ref[0])
            x = _layernorm(DEEPNORM_ALPHA * x + mix, mix_g_ref[...], mix_b_ref[...])
        xs.append(x)
    for x in xs:
        xb = x.astype(_BF16)
        gate = _dot(xb, wg_ref[0])
        up = _dot(xb, wu_ref[0])
        acts.append((gate * jax.nn.sigmoid(gate) * up).astype(_BF16))
    for rows, x, act in zip(groups, xs, acts):
        y = _dot(act, wd_ref[0])
        y_ref[rows, :] = _layernorm(DEEPNORM_ALPHA * x + y, g_ref[...], b_ref[...])


def _ffn(x2d, w_gate, w_up, w_down, g, b, out_proj=None):
    m = x2d.shape[0]
    row_spec = pl.BlockSpec((FFN_TM, D_MODEL), lambda i: (i, 0))
    vec_spec = pl.BlockSpec((1, D_MODEL), lambda i: (0, 0))
    in_specs = [row_spec, _resident((1, D_MODEL, D_FF)), _resident((1, D_MODEL, D_FF)),
                _resident((1, D_FF, D_MODEL)), vec_spec, vec_spec]
    args = [x2d, w_gate, w_up, w_down, g, b]
    if out_proj is not None:
        in_specs = [row_spec, _resident((1, D_MODEL, D_MODEL)), vec_spec, vec_spec] + in_specs
        args = list(out_proj) + args
    return pl.pallas_call(
        functools.partial(_ffn_kernel, with_out_proj=out_proj is not None),
        grid=(m // FFN_TM,),
        in_specs=in_specs,
        out_specs=row_spec,
        out_shape=jax.ShapeDtypeStruct((m, D_MODEL), _F32),
        compiler_params=_params(),
        name="ffn_ln",
    )(*args)


def _gmlp_kernel(x_ref, xnext_ref, win_ref, bin_ref, lng_ref, lnb_ref, ws_ref, bs_ref,
                 wout_ref, g_ref, b_ref, y_ref, za_ref, zb_ref, gated_a_ref, gated_b_ref):
    tm = GMLP_TM
    row = lax.broadcasted_iota(jnp.int32, (GMLP_BLOCK, GMLP_BLOCK), 0)
    col = lax.broadcasted_iota(jnp.int32, (GMLP_BLOCK, GMLP_BLOCK), 1)
    tri = (row >= col).astype(_F32)
    bs = bs_ref[...]

    def stage1(x, z_ref):
        z = _dot(x.astype(_BF16), win_ref[...]) + bin_ref[...]
        z_ref[...] = 0.5 * z * (1.0 + lax.erf(z * (2.0 ** -0.5)))

    def stage2(x, z_ref, gated_ref, out_rows):
        v = _layernorm(z_ref[:, GMLP_HALF:], lng_ref[...], lnb_ref[...]).astype(_BF16)
        for grp in range(GMLP_GROUPS):
            w_mix = (ws_ref[grp] * tri).astype(_BF16)
            cols = slice(grp * GMLP_GROUP_DIM, (grp + 1) * GMLP_GROUP_DIM)
            for blk in range(tm // GMLP_BLOCK):
                rows = slice(blk * GMLP_BLOCK, (blk + 1) * GMLP_BLOCK)
                mixed = _dot(w_mix, v[rows, cols]) + bs[:, grp:grp + 1]
                gated_ref[rows, cols] = (z_ref[rows, cols] * mixed).astype(_BF16)
        y = _dot(gated_ref[...], wout_ref[...])
        y_ref[out_rows, :] = _layernorm(DEEPNORM_ALPHA * x + y, g_ref[...], b_ref[...])

    @pl.when(pl.program_id(0) == 0)
    def _():
        stage1(x_ref[0:tm, :], za_ref)

    stage1(x_ref[tm:2 * tm, :], zb_ref)
    stage2(x_ref[0:tm, :], za_ref, gated_a_ref, slice(0, tm))
    stage1(xnext_ref[...], za_ref)
    stage2(x_ref[tm:2 * tm, :], zb_ref, gated_b_ref, slice(tm, 2 * tm))


def _gmlp(x2d, w_in, b_in, ln_g, ln_b, w_s, b_s_t, w_out, g, b):
    m = x2d.shape[0]
    steps = m // (2 * GMLP_TM)
    pair_spec = pl.BlockSpec((2 * GMLP_TM, D_MODEL), lambda i: (i, 0))
    next_spec = pl.BlockSpec((GMLP_TM, D_MODEL),
                             lambda i: (jnp.minimum(2 * i + 2, 2 * steps - 1), 0))
    vec_spec = pl.BlockSpec((1, D_MODEL), lambda i: (0, 0))
    half_spec = pl.BlockSpec((1, GMLP_HALF), lambda i: (0, 0))
    return pl.pallas_call(
        _gmlp_kernel,
        grid=(steps,),
        in_specs=[pair_spec, next_spec,
                  _resident((D_MODEL, 2 * GMLP_HALF)),
                  pl.BlockSpec((1, 2 * GMLP_HALF), lambda i: (0, 0)),
                  half_spec, half_spec,
                  _resident((GMLP_GROUPS, GMLP_BLOCK, GMLP_BLOCK)),
                  pl.BlockSpec((GMLP_BLOCK, GMLP_GROUPS), lambda i: (0, 0)),
                  _resident((GMLP_HALF, D_MODEL)),
                  vec_spec, vec_spec],
        out_specs=pair_spec,
        out_shape=jax.ShapeDtypeStruct((m, D_MODEL), _F32),
        scratch_shapes=[pltpu.VMEM((GMLP_TM, 2 * GMLP_HALF), _F32),
                        pltpu.VMEM((GMLP_TM, 2 * GMLP_HALF), _F32),
                        pltpu.VMEM((GMLP_TM, GMLP_HALF), _BF16),
                        pltpu.VMEM((GMLP_TM, GMLP_HALF), _BF16)],
        compiler_params=_params(),
        name="gmlp_ln",
    )(x2d, x2d, w_in, b_in, ln_g, ln_b, w_s, b_s_t, w_out, g, b)


def kernel(x, attn_w_qkv, attn_lambda_q1, attn_lambda_k1, attn_lambda_q2, attn_lambda_k2,
           attn_subln_g, attn_w_o, gmlp_w_in, gmlp_b_in, gmlp_ln_g, gmlp_ln_b, gmlp_w_s,
           gmlp_b_s, gmlp_w_out, ln_mix_g, ln_mix_b, ffn_w_gate, ffn_w_up, ffn_w_down,
           ln_ffn_g, ln_ffn_b):
    bsz, seq, d = x.shape
    assert d == D_MODEL and seq % PROJ_TM == 0 and seq % ATTN_T == 0
    m = bsz * seq
    x2d = x.reshape(m, d)
    slopes = 2.0 ** (-8.0 * jnp.arange(1, N_HEADS + 1, dtype=_F32) / N_HEADS)

    def vec(a):
        return a.reshape(1, -1)

    w_qkv = attn_w_qkv[0]
    qt, k, vt, (w_o, w_gate0, w_up0, w_down0, w_gate1, w_up1, w_down1, w_in, w_out) = _qkv_proj(
        x2d, w_qkv[:, :D_MODEL].T.astype(_BF16), w_qkv[:, D_MODEL:2 * D_MODEL].astype(_BF16),
        w_qkv[:, 2 * D_MODEL:].T.astype(_BF16), bsz, seq,
        [(attn_w_o, 0), (ffn_w_gate, 0), (ffn_w_up, 0), (ffn_w_down, 0),
         (ffn_w_gate, 1), (ffn_w_up, 1), (ffn_w_down, 1), (gmlp_w_in, 0), (gmlp_w_out, 0)])
    lam_params = jnp.stack([attn_lambda_q1[0], attn_lambda_k1[0],
                            attn_lambda_q2[0], attn_lambda_k2[0]])
    o = _attention(qt, k.reshape(bsz, seq, d), vt, slopes, lam_params,
                   vec(attn_subln_g[0]), _lambda_init(0))
    x2d = _ffn(x2d, w_gate0, w_up0, w_down0, vec(ln_ffn_g[0]), vec(ln_ffn_b[0]),
               out_proj=(o.reshape(m, d), w_o, vec(ln_mix_g[0]), vec(ln_mix_b[0])))

    x2d = _gmlp(x2d, w_in[0], vec(gmlp_b_in[0]), vec(gmlp_ln_g[0]),
                vec(gmlp_ln_b[0]), gmlp_w_s[0], gmlp_b_s[0].T, w_out[0],
                vec(ln_mix_g[1]), vec(ln_mix_b[1]))
    x2d = _ffn(x2d, w_gate1, w_up1, w_down1, vec(ln_ffn_g[1]), vec(ln_ffn_b[1]))
    return x2d.reshape(bsz, seq, d)
```

```python
import functools
import math
import struct

import jax
import jax.numpy as jnp
from jax import lax
from jax.experimental import pallas as pl
from jax.experimental.pallas import tpu as pltpu

D_MODEL = 1024
DEPTH = 2
CHUNK = 64
N_HEADS = 8
HEAD_DIM_QK = 64
HEAD_DIM_V = 128
GMLP_BLOCK = 128
GMLP_HALF = 2 * D_MODEL
GMLP_GROUPS = 8
GMLP_GROUP_DIM = GMLP_HALF // GMLP_GROUPS
D_FF = 2816
DEEPNORM_ALPHA = (2 * DEPTH) ** 0.25
LN_EPS = 1e-5
MASK_VALUE = -1e30
QK_SCALE = HEAD_DIM_QK ** -0.5
LOG2E = math.log2(math.e)

V7X_VMEM_LIMIT_BYTES = 56 * 1024 * 1024
BF16_SUBLANES = 16

PROJ_TM = 1024
FFN_TM = 1024
FFN_SPLIT = 4
GMLP_TM = 256
ATTN_T = 256
KEY_BLOCK = 512
ATTN_HEADS_PER_STEP = 2
SCORE_SLOTS = 2

_BF16 = jnp.bfloat16
_F32 = jnp.float32


def _bf16_round(x):
    bits = struct.unpack("<I", struct.pack("<f", x))[0]
    bits = (bits + 0x7FFF + ((bits >> 16) & 1)) & 0xFFFF0000
    return struct.unpack("<f", struct.pack("<I", bits))[0]


def _bf16_pieces(x, n=3):
    pieces = []
    for _ in range(n):
        p = _bf16_round(x)
        pieces.append(p)
        x -= p
    return pieces


def _lambda_init(layer_idx):
    return 0.8 - 0.6 * math.exp(-0.3 * layer_idx)


def _layernorm(y, g, b):
    mu = jnp.mean(y, axis=-1, keepdims=True)
    d = y - mu
    var = jnp.mean(d * d, axis=-1, keepdims=True)
    return d * lax.rsqrt(var + LN_EPS) * g + b


def _dot(a, b):
    return jnp.dot(a, b, preferred_element_type=_F32)


def _dot_nt(a, b):
    return lax.dot_general(a, b, (((1,), (1,)), ((), ())), preferred_element_type=_F32)


def _resident(shape):
    return pl.BlockSpec(shape, lambda *_: (0,) * len(shape), pipeline_mode=pl.Buffered(1))


def _params(n_parallel_axes=1, flags=None):
    return pltpu.CompilerParams(
        dimension_semantics=("arbitrary",) * n_parallel_axes,
        vmem_limit_bytes=V7X_VMEM_LIMIT_BYTES,
        flags=flags,
    )


def _cast_plan(weights, steps, step_of):
    in_specs, out_specs, out_shapes = [], [], []
    for w, layer in weights:
        _, rows, cols = w.shape
        slices = steps
        while rows % (slices * BF16_SUBLANES):
            assert slices % 2 == 0, (w.shape, steps)
            slices //= 2
        hold = steps // slices
        block = (1, rows // slices, cols)
        in_specs.append(pl.BlockSpec(
            block, lambda *g, layer=layer, hold=hold: (layer, step_of(*g) // hold, 0)))
        out_specs.append(pl.BlockSpec(block, lambda *g, hold=hold: (0, step_of(*g) // hold, 0)))
        out_shapes.append(jax.ShapeDtypeStruct((1, rows, cols), _BF16))
    return in_specs, out_specs, out_shapes


def _run_casts(src_refs, dst_refs):
    for src, dst in zip(src_refs, dst_refs):
        dst[...] = src[...].astype(_BF16)


def _qkv_kernel(x_ref, wqt_ref, wk_ref, wvt_ref, *refs, n_cast):
    cast_src = refs[:n_cast]
    qt_ref, k_ref, vt_ref = refs[n_cast:n_cast + 3]
    cast_dst = refs[n_cast + 3:]
    xb = x_ref[...].astype(_BF16)
    qt_ref[0] = (_dot_nt(wqt_ref[...], xb) * (QK_SCALE * LOG2E)).astype(_BF16)
    k_ref[...] = _dot(xb, wk_ref[...]).astype(_BF16)
    vt_ref[0] = _dot_nt(wvt_ref[...], xb).astype(_BF16)
    _run_casts(cast_src, cast_dst)


def _qkv_proj(x2d, w_q_t, w_k, w_v_t, bsz, seq, cast_weights):
    m = x2d.shape[0]
    steps = m // PROJ_TM
    tiles_per_seq = seq // PROJ_TM
    row_spec = pl.BlockSpec((PROJ_TM, D_MODEL), lambda i: (i, 0))
    t_spec = pl.BlockSpec((1, D_MODEL, PROJ_TM),
                          lambda i: (i // tiles_per_seq, 0, i % tiles_per_seq))
    w_spec = _resident((D_MODEL, D_MODEL))
    t_shape = jax.ShapeDtypeStruct((bsz, D_MODEL, seq), _BF16)
    cast_in, cast_out, cast_shapes = _cast_plan(cast_weights, steps, lambda i: i)

    outs = pl.pallas_call(
        functools.partial(_qkv_kernel, n_cast=len(cast_weights)),
        grid=(steps,),
        in_specs=[row_spec, w_spec, w_spec, w_spec] + cast_in,
        out_specs=[t_spec, row_spec, t_spec] + cast_out,
        out_shape=[t_shape, jax.ShapeDtypeStruct((m, D_MODEL), _BF16), t_shape] + cast_shapes,
        compiler_params=_params(),
        name="qkv_proj",
    )(x2d, w_q_t, w_k, w_v_t, *[w for w, _ in cast_weights])
    return outs[0], outs[1], outs[2], outs[3:]


_LOG2E_PIECES = _bf16_pieces(LOG2E)


def _attn_kernel(slopes_ref, lam_ref, qt_ref, k_ref, vt_ref, g_ref, o_ref,
                 kaug_ref, qs_ref, s_ref, *, lambda_init):
    T = ATTN_T
    DQ = 2 * HEAD_DIM_QK
    DV = HEAD_DIM_V
    G = ATTN_HEADS_PER_STEP
    seq = k_ref.shape[1]
    n_tiles = seq // T
    slopes = [slopes_ref[pl.program_id(1) * G + g] for g in range(G)]

    pos = lax.broadcasted_iota(jnp.int32, (seq, DQ), 0)
    lane_k = lax.broadcasted_iota(jnp.int32, (seq, DQ), 1)
    j_hi = jnp.where(lane_k < 3, (pos // CHUNK).astype(_F32), 0.0)
    j_lo = jnp.where((lane_k >= 3) & (lane_k < 6), (pos % CHUNK).astype(_F32), 0.0)
    for g in range(G):
        kaug_ref[g, :, 0:DQ] = k_ref[0, :, g * DQ:(g + 1) * DQ]
        kaug_ref[g, :, DQ:2 * DQ] = ((j_hi + j_lo) * slopes[g]).astype(_BF16)

    row_q = lax.broadcasted_iota(jnp.int32, (DQ, 2 * T), 0)
    query_aug = jnp.zeros((DQ, 2 * T), _F32)
    for idx, piece in enumerate(_LOG2E_PIECES):
        query_aug = jnp.where(row_q == idx, CHUNK * piece, query_aug)
        query_aug = jnp.where(row_q == idx + 3, piece, query_aug)
    query_aug = query_aug.astype(_BF16)

    lp = lam_ref[...]
    lam = (jnp.exp(jnp.sum(lp[0:1] * lp[1:2], axis=-1, keepdims=True))
           - jnp.exp(jnp.sum(lp[2:3] * lp[3:4], axis=-1, keepdims=True))
           + lambda_init)

    key_i = lax.broadcasted_iota(jnp.int32, (T, T), 0)
    qry_i = lax.broadcasted_iota(jnp.int32, (T, T), 1)
    ahead = jnp.maximum(key_i - qry_i, 0).astype(_F32)
    allowed = (key_i // CHUNK) <= (qry_i // CHUNK)

    def diag_add(g):
        d = jnp.where(allowed, (-2.0 * LOG2E * slopes[g]) * ahead, MASK_VALUE)
        return jnp.concatenate([d, d], axis=1)

    row_d = lax.broadcasted_iota(jnp.int32, (DQ, T), 0)
    g_scaled = g_ref[...] * (1.0 - lambda_init)

    def stage_queries(g, t, slot):
        qt = qt_ref[0, g * DQ:(g + 1) * DQ, t * T:(t + 1) * T]
        zero = jnp.zeros_like(qt)
        top = jnp.concatenate([jnp.where(row_d < HEAD_DIM_QK, qt, zero),
                               jnp.where(row_d >= HEAD_DIM_QK, qt, zero)], axis=1)
        qs_ref[slot] = jnp.concatenate([top, query_aug], axis=0)

    def key_blocks(t):
        edges = list(range(0, t * T, KEY_BLOCK)) + [t * T]
        return [(lo, hi, False) for lo, hi in zip(edges[:-1], edges[1:])] + \
               [(t * T, (t + 1) * T, True)]

    def score_pass(g, t, slot, blk, m):
        lo, hi, is_diag = blk
        rows = slice(lo, hi)
        s = jnp.dot(kaug_ref[g, rows, :], qs_ref[slot],
                    preferred_element_type=_F32)
        if is_diag:
            s = s + diag_add(g)
        s_ref[slot, rows, :] = s
        m_blk = jnp.max(s, axis=0, keepdims=True)
        return m_blk if m is None else jnp.maximum(m, m_blk)

    def value_pass(g, slot, blk, m, l, acc):
        rows = slice(blk[0], blk[1])
        p = jnp.exp2(s_ref[slot, rows, :] - m)
        l_blk = jnp.sum(p, axis=0, keepdims=True)
        acc_blk = jnp.dot(vt_ref[0, g * DV:(g + 1) * DV, rows], p.astype(_BF16),
                          preferred_element_type=_F32)
        if l is None:
            return l_blk, acc_blk
        return l + l_blk, acc + acc_blk

    def finish(g, t, l, acc):
        att = acc / l
        o = (att[:, 0:T] - lam * att[:, T:2 * T]).T
        ms = jnp.mean(o * o, axis=-1, keepdims=True)
        o = o * lax.rsqrt(ms + LN_EPS) * g_scaled
        o_ref[0, t * T:(t + 1) * T, g * DV:(g + 1) * DV] = o.astype(_BF16)

    work = []
    for g in range(G):
        tiles = range(n_tiles) if g % 2 == 0 else range(n_tiles - 1, -1, -1)
        work += [(g, t) for t in tiles]
    col_max = {}
    for i in range(len(work) + 1):
        if i < len(work):
            g, t = work[i]
            stage_queries(g, t, i % SCORE_SLOTS)
            m = None
            for blk in key_blocks(t):
                m = score_pass(g, t, i % SCORE_SLOTS, blk, m)
            col_max[i] = m
        if i > 0:
            g, t = work[i - 1]
            l, acc = None, None
            for blk in key_blocks(t):
                l, acc = value_pass(g, (i - 1) % SCORE_SLOTS, blk, col_max[i - 1], l, acc)
            finish(g, t, l, acc)


def _attention(qt, k, vt, slopes, lam_params, g_sub, lambda_init):
    b, s, _ = k.shape
    width = ATTN_HEADS_PER_STEP * HEAD_DIM_V
    head_spec = pl.BlockSpec((1, s, width), lambda bi, hi: (bi, 0, hi))
    head_t_spec = pl.BlockSpec((1, width, s), lambda bi, hi: (bi, hi, 0))
    return pl.pallas_call(
        functools.partial(_attn_kernel, lambda_init=lambda_init),
        grid=(b, N_HEADS // ATTN_HEADS_PER_STEP),
        in_specs=[
            pl.BlockSpec(memory_space=pltpu.SMEM),
            pl.BlockSpec((4, HEAD_DIM_QK), lambda bi, hi: (0, 0)),
            head_t_spec, head_spec, head_t_spec,
            pl.BlockSpec((1, HEAD_DIM_V), lambda bi, hi: (0, 0)),
        ],
        out_specs=head_spec,
        out_shape=jax.ShapeDtypeStruct((b, s, N_HEADS * HEAD_DIM_V), _BF16),
        scratch_shapes=[
            pltpu.VMEM((ATTN_HEADS_PER_STEP, s, 4 * HEAD_DIM_QK), _BF16),
            pltpu.VMEM((SCORE_SLOTS, 4 * HEAD_DIM_QK, 2 * ATTN_T), _BF16),
            pltpu.VMEM((SCORE_SLOTS, s, 2 * ATTN_T), _F32),
        ],
        compiler_params=_params(2),
        name="diff_attention",
    )(slopes, lam_params, qt, k, vt, g_sub)


def _ffn_kernel(*refs, with_out_proj, n_cast):
    if with_out_proj:
        o_ref, wo_ref, mix_g_ref, mix_b_ref = refs[:4]
        refs = refs[4:]
    x_ref, wg_ref, wu_ref, wd_ref, g_ref, b_ref = refs[:6]
    cast_src = refs[6:6 + n_cast]
    y_ref = refs[6 + n_cast]
    _run_casts(cast_src, refs[7 + n_cast:])
    rows_per = FFN_TM // FFN_SPLIT
    groups = [slice(i * rows_per, (i + 1) * rows_per) for i in range(FFN_SPLIT)]
    xs, acts = [], []
    for rows in groups:
        x = x_ref[rows, :]
        if with_out_proj:
            mix = _dot(o_ref[rows, :], wo_ref[0])
            x = _layernorm(DEEPNORM_ALPHA * x + mix, mix_g_ref[...], mix_b_ref[...])
        xs.append(x)
    for x in xs:
        xb = x.astype(_BF16)
        gate = _dot(xb, wg_ref[0])
        up = _dot(xb, wu_ref[0])
        acts.append((gate * jax.nn.sigmoid(gate) * up).astype(_BF16))
    for rows, x, act in zip(groups, xs, acts):
        y = _dot(act, wd_ref[0])
        y_ref[rows, :] = _layernorm(DEEPNORM_ALPHA * x + y, g_ref[...], b_ref[...])


def _ffn(x2d, w_gate, w_up, w_down, g, b, out_proj=None, cast_weights=()):
    m = x2d.shape[0]
    steps = m // FFN_TM
    row_spec = pl.BlockSpec((FFN_TM, D_MODEL), lambda i: (i, 0))
    vec_spec = pl.BlockSpec((1, D_MODEL), lambda i: (0, 0))
    in_specs = [row_spec, _resident((1, D_MODEL, D_FF)), _resident((1, D_MODEL, D_FF)),
                _resident((1, D_FF, D_MODEL)), vec_spec, vec_spec]
    args = [x2d, w_gate, w_up, w_down, g, b]
    if out_proj is not None:
        in_specs = [row_spec, _resident((1, D_MODEL, D_MODEL)), vec_spec, vec_spec] + in_specs
        args = list(out_proj) + args
    cast_in, cast_out, cast_shapes = _cast_plan(cast_weights, steps, lambda i: i)
    outs = pl.pallas_call(
        functools.partial(_ffn_kernel, with_out_proj=out_proj is not None,
                          n_cast=len(cast_weights)),
        grid=(steps,),
        in_specs=in_specs + cast_in,
        out_specs=[row_spec] + cast_out,
        out_shape=[jax.ShapeDtypeStruct((m, D_MODEL), _F32)] + cast_shapes,
        compiler_params=_params(),
        name="ffn_ln",
    )(*args, *[w for w, _ in cast_weights])
    return outs[0], outs[1:]


def _gmlp_kernel(x_ref, win_ref, bin_ref, lng_ref, lnb_ref, ws_ref, bs_ref,
                 wout_ref, g_ref, b_ref, *refs, n_cast):
    cast_src = refs[:n_cast]
    y_ref = refs[n_cast]
    cast_dst = refs[n_cast + 1:2 * n_cast + 1]
    za_ref, zb_ref, gated_a_ref, gated_b_ref = refs[2 * n_cast + 1:]
    _run_casts(cast_src, cast_dst)
    tm = GMLP_TM
    row = lax.broadcasted_iota(jnp.int32, (GMLP_BLOCK, GMLP_BLOCK), 0)
    col = lax.broadcasted_iota(jnp.int32, (GMLP_BLOCK, GMLP_BLOCK), 1)
    tri = (row >= col).astype(_F32)
    bs = bs_ref[...]

    def stage1(x, z_ref):
        z = _dot(x.astype(_BF16), win_ref[...]) + bin_ref[...]
        z_ref[...] = 0.5 * z * (1.0 + lax.erf(z * (2.0 ** -0.5)))

    def stage2a(z_ref, gated_ref):
        v = _layernorm(z_ref[:, GMLP_HALF:], lng_ref[...], lnb_ref[...]).astype(_BF16)
        for grp in range(GMLP_GROUPS):
            w_mix = (ws_ref[grp] * tri).astype(_BF16)
            cols = slice(grp * GMLP_GROUP_DIM, (grp + 1) * GMLP_GROUP_DIM)
            for blk in range(tm // GMLP_BLOCK):
                rows = slice(blk * GMLP_BLOCK, (blk + 1) * GMLP_BLOCK)
                mixed = _dot(w_mix, v[rows, cols]) + bs[:, grp:grp + 1]
                gated_ref[rows, cols] = (z_ref[rows, cols] * mixed).astype(_BF16)

    def stage2b(x, gated_ref, out_rows):
        y = _dot(gated_ref[...], wout_ref[...])
        y_ref[out_rows, :] = _layernorm(DEEPNORM_ALPHA * x + y, g_ref[...], b_ref[...])

    stage1(x_ref[0:tm, :], za_ref)
    stage1(x_ref[tm:2 * tm, :], zb_ref)
    stage2a(za_ref, gated_a_ref)
    stage2b(x_ref[0:tm, :], gated_a_ref, slice(0, tm))
    stage2a(zb_ref, gated_b_ref)
    stage2b(x_ref[tm:2 * tm, :], gated_b_ref, slice(tm, 2 * tm))


def _gmlp(x2d, w_in, b_in, ln_g, ln_b, w_s, b_s_t, w_out, g, b, cast_weights=()):
    m = x2d.shape[0]
    steps = m // (2 * GMLP_TM)
    pair_spec = pl.BlockSpec((2 * GMLP_TM, D_MODEL), lambda i: (i, 0))
    vec_spec = pl.BlockSpec((1, D_MODEL), lambda i: (0, 0))
    half_spec = pl.BlockSpec((1, GMLP_HALF), lambda i: (0, 0))
    cast_in, cast_out, cast_shapes = _cast_plan(cast_weights, steps, lambda i: i)
    outs = pl.pallas_call(
        functools.partial(_gmlp_kernel, n_cast=len(cast_weights)),
        grid=(steps,),
        in_specs=[pair_spec,
                  _resident((D_MODEL, 2 * GMLP_HALF)),
                  pl.BlockSpec((1, 2 * GMLP_HALF), lambda i: (0, 0)),
                  half_spec, half_spec,
                  _resident((GMLP_GROUPS, GMLP_BLOCK, GMLP_BLOCK)),
                  pl.BlockSpec((GMLP_BLOCK, GMLP_GROUPS), lambda i: (0, 0)),
                  _resident((GMLP_HALF, D_MODEL)),
                  vec_spec, vec_spec] + cast_in,
        out_specs=[pair_spec] + cast_out,
        out_shape=[jax.ShapeDtypeStruct((m, D_MODEL), _F32)] + cast_shapes,
        scratch_shapes=[pltpu.VMEM((GMLP_TM, 2 * GMLP_HALF), _F32),
                        pltpu.VMEM((GMLP_TM, 2 * GMLP_HALF), _F32),
                        pltpu.VMEM((GMLP_TM, GMLP_HALF), _BF16),
                        pltpu.VMEM((GMLP_TM, GMLP_HALF), _BF16)],
        compiler_params=_params(),
        name="gmlp_ln",
    )(x2d, w_in, b_in, ln_g, ln_b, w_s, b_s_t, w_out, g, b, *[w for w, _ in cast_weights])
    return outs[0], outs[1:]


def kernel(x, attn_w_qkv, attn_lambda_q1, attn_lambda_k1, attn_lambda_q2, attn_lambda_k2,
           attn_subln_g, attn_w_o, gmlp_w_in, gmlp_b_in, gmlp_ln_g, gmlp_ln_b, gmlp_w_s,
           gmlp_b_s, gmlp_w_out, ln_mix_g, ln_mix_b, ffn_w_gate, ffn_w_up, ffn_w_down,
           ln_ffn_g, ln_ffn_b):
    bsz, seq, d = x.shape
    assert d == D_MODEL and seq % PROJ_TM == 0 and seq % ATTN_T == 0
    m = bsz * seq
    x2d = x.reshape(m, d)
    slopes = 2.0 ** (-8.0 * jnp.arange(1, N_HEADS + 1, dtype=_F32) / N_HEADS)

    def vec(a):
        return a.reshape(1, -1)

    w_qkv = attn_w_qkv[0]
    qt, k, vt, (w_o, w_gate0, w_up0, w_down0) = _qkv_proj(
        x2d, w_qkv[:, :D_MODEL].T.astype(_BF16), w_qkv[:, D_MODEL:2 * D_MODEL].astype(_BF16),
        w_qkv[:, 2 * D_MODEL:].T.astype(_BF16), bsz, seq,
        [(attn_w_o, 0), (ffn_w_gate, 0), (ffn_w_up, 0), (ffn_w_down, 0)])
    lam_params = jnp.stack([attn_lambda_q1[0], attn_lambda_k1[0],
                            attn_lambda_q2[0], attn_lambda_k2[0]])
    o = _attention(qt, k.reshape(bsz, seq, d), vt, slopes, lam_params,
                   vec(attn_subln_g[0]), _lambda_init(0))
    x2d, (w_in, w_out) = _ffn(
        x2d, w_gate0, w_up0, w_down0, vec(ln_ffn_g[0]), vec(ln_ffn_b[0]),
        out_proj=(o.reshape(m, d), w_o, vec(ln_mix_g[0]), vec(ln_mix_b[0])),
        cast_weights=[(gmlp_w_in, 0), (gmlp_w_out, 0)])

    x2d, (w_gate1, w_up1, w_down1) = _gmlp(
        x2d, w_in[0], vec(gmlp_b_in[0]), vec(gmlp_ln_g[0]), vec(gmlp_ln_b[0]), gmlp_w_s[0],
        gmlp_b_s[0].T, w_out[0], vec(ln_mix_g[1]), vec(ln_mix_b[1]),
        cast_weights=[(ffn_w_gate, 1), (ffn_w_up, 1), (ffn_w_down, 1)])
    x2d, _ = _ffn(x2d, w_gate1, w_up1, w_down1, vec(ln_ffn_g[1]), vec(ln_ffn_b[1]))
    return x2d.reshape(bsz, seq, d)
```

```python
import functools
import math
import struct

import jax
import jax.numpy as jnp
from jax import lax
from jax.experimental import pallas as pl
from jax.experimental.pallas import tpu as pltpu

D_MODEL = 1024
DEPTH = 2
CHUNK = 64
N_HEADS = 8
HEAD_DIM_QK = 64
HEAD_DIM_V = 128
GMLP_BLOCK = 128
GMLP_HALF = 2 * D_MODEL
GMLP_GROUPS = 8
GMLP_GROUP_DIM = GMLP_HALF // GMLP_GROUPS
D_FF = 2816
DEEPNORM_ALPHA = (2 * DEPTH) ** 0.25
LN_EPS = 1e-5
MASK_VALUE = -1e30
QK_SCALE = HEAD_DIM_QK ** -0.5
LOG2E = math.log2(math.e)

V7X_VMEM_LIMIT_BYTES = 56 * 1024 * 1024
BF16_SUBLANES = 16

PROJ_TM = 1024
FFN_TM = 1024
FFN_SPLIT = 4
GMLP_TM = 256
ATTN_T = 256
KEY_BLOCK = 512
ATTN_HEADS_PER_STEP = 2
SCORE_SLOTS = 2

_BF16 = jnp.bfloat16
_F32 = jnp.float32


def _bf16_round(x):
    bits = struct.unpack("<I", struct.pack("<f", x))[0]
    bits = (bits + 0x7FFF + ((bits >> 16) & 1)) & 0xFFFF0000
    return struct.unpack("<f", struct.pack("<I", bits))[0]


def _bf16_pieces(x, n=3):
    pieces = []
    for _ in range(n):
        p = _bf16_round(x)
        pieces.append(p)
        x -= p
    return pieces


def _lambda_init(layer_idx):
    return 0.8 - 0.6 * math.exp(-0.3 * layer_idx)


def _layernorm(y, g, b):
    mu = jnp.mean(y, axis=-1, keepdims=True)
    d = y - mu
    var = jnp.mean(d * d, axis=-1, keepdims=True)
    return d * lax.rsqrt(var + LN_EPS) * g + b


def _dot(a, b):
    return jnp.dot(a, b, preferred_element_type=_F32)


def _dot_nt(a, b):
    return lax.dot_general(a, b, (((1,), (1,)), ((), ())), preferred_element_type=_F32)


def _resident(shape):
    return pl.BlockSpec(shape, lambda *_: (0,) * len(shape), pipeline_mode=pl.Buffered(1))


def _params(n_parallel_axes=1, flags=None):
    return pltpu.CompilerParams(
        dimension_semantics=("arbitrary",) * n_parallel_axes,
        vmem_limit_bytes=V7X_VMEM_LIMIT_BYTES,
        flags=flags,
    )


def _cast_plan(weights, steps, step_of):
    in_specs, out_specs, out_shapes = [], [], []
    for w, layer in weights:
        _, rows, cols = w.shape
        slices = steps
        while rows % (slices * BF16_SUBLANES):
            assert slices % 2 == 0, (w.shape, steps)
            slices //= 2
        hold = steps // slices
        block = (1, rows // slices, cols)
        in_specs.append(pl.BlockSpec(
            block, lambda *g, layer=layer, hold=hold: (layer, step_of(*g) // hold, 0)))
        out_specs.append(pl.BlockSpec(block, lambda *g, hold=hold: (0, step_of(*g) // hold, 0)))
        out_shapes.append(jax.ShapeDtypeStruct((1, rows, cols), _BF16))
    return in_specs, out_specs, out_shapes


def _run_casts(src_refs, dst_refs):
    for src, dst in zip(src_refs, dst_refs):
        dst[...] = src[...].astype(_BF16)


def _qkv_kernel(x_ref, wqt_ref, wk_ref, wvt_ref, *refs, n_cast):
    cast_src = refs[:n_cast]
    qt_ref, k_ref, vt_ref = refs[n_cast:n_cast + 3]
    cast_dst = refs[n_cast + 3:]
    xb = x_ref[...].astype(_BF16)
    qt_ref[0] = (_dot_nt(wqt_ref[...], xb) * (QK_SCALE * LOG2E)).astype(_BF16)
    k_ref[...] = _dot(xb, wk_ref[...]).astype(_BF16)
    vt_ref[0] = _dot_nt(wvt_ref[...], xb).astype(_BF16)
    _run_casts(cast_src, cast_dst)


def _qkv_proj(x2d, w_q_t, w_k, w_v_t, bsz, seq, cast_weights):
    m = x2d.shape[0]
    steps = m // PROJ_TM
    tiles_per_seq = seq // PROJ_TM
    row_spec = pl.BlockSpec((PROJ_TM, D_MODEL), lambda i: (i, 0))
    t_spec = pl.BlockSpec((1, D_MODEL, PROJ_TM),
                          lambda i: (i // tiles_per_seq, 0, i % tiles_per_seq))
    w_spec = _resident((D_MODEL, D_MODEL))
    t_shape = jax.ShapeDtypeStruct((bsz, D_MODEL, seq), _BF16)
    cast_in, cast_out, cast_shapes = _cast_plan(cast_weights, steps, lambda i: i)

    outs = pl.pallas_call(
        functools.partial(_qkv_kernel, n_cast=len(cast_weights)),
        grid=(steps,),
        in_specs=[row_spec, w_spec, w_spec, w_spec] + cast_in,
        out_specs=[t_spec, row_spec, t_spec] + cast_out,
        out_shape=[t_shape, jax.ShapeDtypeStruct((m, D_MODEL), _BF16), t_shape] + cast_shapes,
        compiler_params=_params(),
        name="qkv_proj",
    )(x2d, w_q_t, w_k, w_v_t, *[w for w, _ in cast_weights])
    return outs[0], outs[1], outs[2], outs[3:]


_LOG2E_PIECES = _bf16_pieces(LOG2E)


def _attn_kernel(slopes_ref, lam_ref, qt_ref, k_ref, vt_ref, g_ref, o_ref,
                 kaug_ref, qs_ref, s_ref, *, lambda_init):
    T = ATTN_T
    DQ = 2 * HEAD_DIM_QK
    DV = HEAD_DIM_V
    G = ATTN_HEADS_PER_STEP
    seq = k_ref.shape[1]
    n_tiles = seq // T
    slopes = [slopes_ref[pl.program_id(1) * G + g] for g in range(G)]

    NP = len(_LOG2E_PIECES)
    pos = lax.broadcasted_iota(jnp.int32, (seq, DQ), 0)
    lane_k = lax.broadcasted_iota(jnp.int32, (seq, DQ), 1)
    j_hi = jnp.where(lane_k < NP, (pos // CHUNK).astype(_F32), 0.0)
    j_lo = jnp.where((lane_k >= NP) & (lane_k < 2 * NP), (pos % CHUNK).astype(_F32), 0.0)
    for g in range(G):
        kaug_ref[g, :, 0:DQ] = k_ref[0, :, g * DQ:(g + 1) * DQ]
        kaug_ref[g, :, DQ:2 * DQ] = ((j_hi + j_lo) * slopes[g]).astype(_BF16)

    row_q = lax.broadcasted_iota(jnp.int32, (DQ, 2 * T), 0)
    query_aug = jnp.zeros((DQ, 2 * T), _F32)
    for idx, piece in enumerate(_LOG2E_PIECES):
        query_aug = jnp.where(row_q == idx, CHUNK * piece, query_aug)
        query_aug = jnp.where(row_q == idx + NP, piece, query_aug)
    query_aug = query_aug.astype(_BF16)

    lp = lam_ref[...]
    lam = (jnp.exp(jnp.sum(lp[0:1] * lp[1:2], axis=-1, keepdims=True))
           - jnp.exp(jnp.sum(lp[2:3] * lp[3:4], axis=-1, keepdims=True))
           + lambda_init)

    key_i = lax.broadcasted_iota(jnp.int32, (T, T), 0)
    qry_i = lax.broadcasted_iota(jnp.int32, (T, T), 1)
    ahead = jnp.maximum(key_i - qry_i, 0).astype(_F32)
    allowed = (key_i // CHUNK) <= (qry_i // CHUNK)

    diag_adds = []
    for g in range(G):
        d = jnp.where(allowed, (-2.0 * LOG2E * slopes[g]) * ahead, MASK_VALUE)
        diag_adds.append(jnp.concatenate([d, d], axis=1))

    row_d = lax.broadcasted_iota(jnp.int32, (DQ, T), 0)
    g_scaled = g_ref[...] * (1.0 - lambda_init)

    def stage_queries(g, t, slot):
        qt = qt_ref[0, g * DQ:(g + 1) * DQ, t * T:(t + 1) * T]
        zero = jnp.zeros_like(qt)
        top = jnp.concatenate([jnp.where(row_d < HEAD_DIM_QK, qt, zero),
                               jnp.where(row_d >= HEAD_DIM_QK, qt, zero)], axis=1)
        qs_ref[slot] = jnp.concatenate([top, query_aug], axis=0)

    def key_blocks(t):
        edges = list(range(0, t * T, KEY_BLOCK)) + [t * T]
        return [(lo, hi, False) for lo, hi in zip(edges[:-1], edges[1:])] + \
               [(t * T, (t + 1) * T, True)]

    def score_pass(g, t, slot, blk, m):
        lo, hi, is_diag = blk
        rows = slice(lo, hi)
        s = jnp.dot(kaug_ref[g, rows, :], qs_ref[slot],
                    preferred_element_type=_F32)
        if is_diag:
            s = s + diag_adds[g]
        s_ref[slot, rows, :] = s
        m_blk = jnp.max(s, axis=0, keepdims=True)
        return m_blk if m is None else jnp.maximum(m, m_blk)

    def value_pass(g, slot, blk, m, l, acc):
        rows = slice(blk[0], blk[1])
        p = jnp.exp2(s_ref[slot, rows, :] - m)
        l_blk = jnp.sum(p, axis=0, keepdims=True)
        acc_blk = jnp.dot(vt_ref[0, g * DV:(g + 1) * DV, rows], p.astype(_BF16),
                          preferred_element_type=_F32)
        if l is None:
            return l_blk, acc_blk
        return l + l_blk, acc + acc_blk

    def finish(g, t, l, acc):
        att = acc / l
        o = (att[:, 0:T] - lam * att[:, T:2 * T]).T
        ms = jnp.mean(o * o, axis=-1, keepdims=True)
        o = o * lax.rsqrt(ms + LN_EPS) * g_scaled
        o_ref[0, t * T:(t + 1) * T, g * DV:(g + 1) * DV] = o.astype(_BF16)

    work = []
    for g in range(G):
        tiles = range(n_tiles) if g % 2 == 0 else range(n_tiles - 1, -1, -1)
        work += [(g, t) for t in tiles]
    col_max = {}
    for i in range(len(work) + 1):
        if i < len(work):
            g, t = work[i]
            stage_queries(g, t, i % SCORE_SLOTS)
            m = None
            for blk in key_blocks(t):
                m = score_pass(g, t, i % SCORE_SLOTS, blk, m)
            col_max[i] = m
        if i > 0:
            g, t = work[i - 1]
            l, acc = None, None
            for blk in key_blocks(t):
                l, acc = value_pass(g, (i - 1) % SCORE_SLOTS, blk, col_max[i - 1], l, acc)
            finish(g, t, l, acc)


def _attention(qt, k, vt, slopes, lam_params, g_sub, lambda_init):
    b, s, _ = k.shape
    width = ATTN_HEADS_PER_STEP * HEAD_DIM_V
    head_spec = pl.BlockSpec((1, s, width), lambda bi, hi: (bi, 0, hi))
    head_t_spec = pl.BlockSpec((1, width, s), lambda bi, hi: (bi, hi, 0))
    return pl.pallas_call(
        functools.partial(_attn_kernel, lambda_init=lambda_init),
        grid=(b, N_HEADS // ATTN_HEADS_PER_STEP),
        in_specs=[
            pl.BlockSpec(memory_space=pltpu.SMEM),
            pl.BlockSpec((4, HEAD_DIM_QK), lambda bi, hi: (0, 0)),
            head_t_spec, head_spec, head_t_spec,
            pl.BlockSpec((1, HEAD_DIM_V), lambda bi, hi: (0, 0)),
        ],
        out_specs=head_spec,
        out_shape=jax.ShapeDtypeStruct((b, s, N_HEADS * HEAD_DIM_V), _BF16),
        scratch_shapes=[
            pltpu.VMEM((ATTN_HEADS_PER_STEP, s, 4 * HEAD_DIM_QK), _BF16),
            pltpu.VMEM((SCORE_SLOTS, 4 * HEAD_DIM_QK, 2 * ATTN_T), _BF16),
            pltpu.VMEM((SCORE_SLOTS, s, 2 * ATTN_T), _F32),
        ],
        compiler_params=_params(2),
        name="diff_attention",
    )(slopes, lam_params, qt, k, vt, g_sub)


def _ffn_kernel(*refs, with_out_proj, n_cast):
    if with_out_proj:
        o_ref, wo_ref, mix_g_ref, mix_b_ref = refs[:4]
        refs = refs[4:]
    x_ref, wg_ref, wu_ref, wd_ref, g_ref, b_ref = refs[:6]
    cast_src = refs[6:6 + n_cast]
    y_ref = refs[6 + n_cast]
    _run_casts(cast_src, refs[7 + n_cast:])
    rows_per = FFN_TM // FFN_SPLIT
    groups = [slice(i * rows_per, (i + 1) * rows_per) for i in range(FFN_SPLIT)]
    xs, acts = [], []
    for rows in groups:
        x = x_ref[rows, :]
        if with_out_proj:
            mix = _dot(o_ref[rows, :], wo_ref[0])
            x = _layernorm(DEEPNORM_ALPHA * x + mix, mix_g_ref[...], mix_b_ref[...])
        xs.append(x)
    for x in xs:
        xb = x.astype(_BF16)
        gate = _dot(xb, wg_ref[0])
        up = _dot(xb, wu_ref[0])
        acts.append((gate * jax.nn.sigmoid(gate) * up).astype(_BF16))
    for rows, x, act in zip(groups, xs, acts):
        y = _dot(act, wd_ref[0])
        y_ref[rows, :] = _layernorm(DEEPNORM_ALPHA * x + y, g_ref[...], b_ref[...])


def _ffn(x2d, w_gate, w_up, w_down, g, b, out_proj=None, cast_weights=()):
    m = x2d.shape[0]
    steps = m // FFN_TM
    row_spec = pl.BlockSpec((FFN_TM, D_MODEL), lambda i: (i, 0))
    vec_spec = pl.BlockSpec((1, D_MODEL), lambda i: (0, 0))
    in_specs = [row_spec, _resident((1, D_MODEL, D_FF)), _resident((1, D_MODEL, D_FF)),
                _resident((1, D_FF, D_MODEL)), vec_spec, vec_spec]
    args = [x2d, w_gate, w_up, w_down, g, b]
    if out_proj is not None:
        in_specs = [row_spec, _resident((1, D_MODEL, D_MODEL)), vec_spec, vec_spec] + in_specs
        args = list(out_proj) + args
    cast_in, cast_out, cast_shapes = _cast_plan(cast_weights, steps, lambda i: i)
    outs = pl.pallas_call(
        functools.partial(_ffn_kernel, with_out_proj=out_proj is not None,
                          n_cast=len(cast_weights)),
        grid=(steps,),
        in_specs=in_specs + cast_in,
        out_specs=[row_spec] + cast_out,
        out_shape=[jax.ShapeDtypeStruct((m, D_MODEL), _F32)] + cast_shapes,
        compiler_params=_params(),
        name="ffn_ln",
    )(*args, *[w for w, _ in cast_weights])
    return outs[0], outs[1:]


def _gmlp_kernel(x_ref, win_ref, bin_ref, lng_ref, lnb_ref, ws_ref, bs_ref,
                 wout_ref, g_ref, b_ref, *refs, n_cast):
    cast_src = refs[:n_cast]
    y_ref = refs[n_cast]
    cast_dst = refs[n_cast + 1:2 * n_cast + 1]
    za_ref, zb_ref, gated_a_ref, gated_b_ref = refs[2 * n_cast + 1:]
    _run_casts(cast_src, cast_dst)
    tm = GMLP_TM
    row = lax.broadcasted_iota(jnp.int32, (GMLP_BLOCK, GMLP_BLOCK), 0)
    col = lax.broadcasted_iota(jnp.int32, (GMLP_BLOCK, GMLP_BLOCK), 1)
    tri = (row >= col).astype(_F32)
    bs = bs_ref[...]

    def stage1(x, z_ref):
        z = _dot(x.astype(_BF16), win_ref[...]) + bin_ref[...]
        z_ref[...] = 0.5 * z * (1.0 + lax.erf(z * (2.0 ** -0.5)))

    def stage2a(z_ref, gated_ref):
        v = _layernorm(z_ref[:, GMLP_HALF:], lng_ref[...], lnb_ref[...]).astype(_BF16)
        for grp in range(GMLP_GROUPS):
            w_mix = (ws_ref[grp] * tri).astype(_BF16)
            cols = slice(grp * GMLP_GROUP_DIM, (grp + 1) * GMLP_GROUP_DIM)
            for blk in range(tm // GMLP_BLOCK):
                rows = slice(blk * GMLP_BLOCK, (blk + 1) * GMLP_BLOCK)
                mixed = _dot(w_mix, v[rows, cols]) + bs[:, grp:grp + 1]
                gated_ref[rows, cols] = (z_ref[rows, cols] * mixed).astype(_BF16)

    def stage2b(x, gated_ref, out_rows):
        y = _dot(gated_ref[...], wout_ref[...])
        y_ref[out_rows, :] = _layernorm(DEEPNORM_ALPHA * x + y, g_ref[...], b_ref[...])

    stage1(x_ref[0:tm, :], za_ref)
    stage1(x_ref[tm:2 * tm, :], zb_ref)
    stage2a(za_ref, gated_a_ref)
    stage2b(x_ref[0:tm, :], gated_a_ref, slice(0, tm))
    stage2a(zb_ref, gated_b_ref)
    stage2b(x_ref[tm:2 * tm, :], gated_b_ref, slice(tm, 2 * tm))


def _gmlp(x2d, w_in, b_in, ln_g, ln_b, w_s, b_s_t, w_out, g, b, cast_weights=()):
    m = x2d.shape[0]
    steps = m // (2 * GMLP_TM)
    pair_spec = pl.BlockSpec((2 * GMLP_TM, D_MODEL), lambda i: (i, 0))
    vec_spec = pl.BlockSpec((1, D_MODEL), lambda i: (0, 0))
    half_spec = pl.BlockSpec((1, GMLP_HALF), lambda i: (0, 0))
    cast_in, cast_out, cast_shapes = _cast_plan(cast_weights, steps, lambda i: i)
    outs = pl.pallas_call(
        functools.partial(_gmlp_kernel, n_cast=len(cast_weights)),
        grid=(steps,),
        in_specs=[pair_spec,
                  _resident((D_MODEL, 2 * GMLP_HALF)),
                  pl.BlockSpec((1, 2 * GMLP_HALF), lambda i: (0, 0)),
                  half_spec, half_spec,
                  _resident((GMLP_GROUPS, GMLP_BLOCK, GMLP_BLOCK)),
                  pl.BlockSpec((GMLP_BLOCK, GMLP_GROUPS), lambda i: (0, 0)),
                  _resident((GMLP_HALF, D_MODEL)),
                  vec_spec, vec_spec] + cast_in,
        out_specs=[pair_spec] + cast_out,
        out_shape=[jax.ShapeDtypeStruct((m, D_MODEL), _F32)] + cast_shapes,
        scratch_shapes=[pltpu.VMEM((GMLP_TM, 2 * GMLP_HALF), _F32),
                        pltpu.VMEM((GMLP_TM, 2 * GMLP_HALF), _F32),
                        pltpu.VMEM((GMLP_TM, GMLP_HALF), _BF16),
                        pltpu.VMEM((GMLP_TM, GMLP_HALF), _BF16)],
        compiler_params=_params(),
        name="gmlp_ln",
    )(x2d, w_in, b_in, ln_g, ln_b, w_s, b_s_t, w_out, g, b, *[w for w, _ in cast_weights])
    return outs[0], outs[1:]


def kernel(x, attn_w_qkv, attn_lambda_q1, attn_lambda_k1, attn_lambda_q2, attn_lambda_k2,
           attn_subln_g, attn_w_o, gmlp_w_in, gmlp_b_in, gmlp_ln_g, gmlp_ln_b, gmlp_w_s,
           gmlp_b_s, gmlp_w_out, ln_mix_g, ln_mix_b, ffn_w_gate, ffn_w_up, ffn_w_down,
           ln_ffn_g, ln_ffn_b):
    bsz, seq, d = x.shape
    assert d == D_MODEL and seq % PROJ_TM == 0 and seq % ATTN_T == 0
    m = bsz * seq
    x2d = x.reshape(m, d)
    slopes = 2.0 ** (-8.0 * jnp.arange(1, N_HEADS + 1, dtype=_F32) / N_HEADS)

    def vec(a):
        return a.reshape(1, -1)

    w_qkv = attn_w_qkv[0]
    qt, k, vt, (w_o, w_gate0, w_up0, w_down0) = _qkv_proj(
        x2d, w_qkv[:, :D_MODEL].T.astype(_BF16), w_qkv[:, D_MODEL:2 * D_MODEL].astype(_BF16),
        w_qkv[:, 2 * D_MODEL:].T.astype(_BF16), bsz, seq,
        [(attn_w_o, 0), (ffn_w_gate, 0), (ffn_w_up, 0), (ffn_w_down, 0)])
    lam_params = jnp.stack([attn_lambda_q1[0], attn_lambda_k1[0],
                            attn_lambda_q2[0], attn_lambda_k2[0]])
    o = _attention(qt, k.reshape(bsz, seq, d), vt, slopes, lam_params,
                   vec(attn_subln_g[0]), _lambda_init(0))
    x2d, (w_in, w_out) = _ffn(
        x2d, w_gate0, w_up0, w_down0, vec(ln_ffn_g[0]), vec(ln_ffn_b[0]),
        out_proj=(o.reshape(m, d), w_o, vec(ln_mix_g[0]), vec(ln_mix_b[0])),
        cast_weights=[(gmlp_w_in, 0), (gmlp_w_out, 0)])

    x2d, (w_gate1, w_up1, w_down1) = _gmlp(
        x2d, w_in[0], vec(gmlp_b_in[0]), vec(gmlp_ln_g[0]), vec(gmlp_ln_b[0]), gmlp_w_s[0],
        gmlp_b_s[0].T, w_out[0], vec(ln_mix_g[1]), vec(ln_mix_b[1]),
        cast_weights=[(ffn_w_gate, 1), (ffn_w_up, 1), (ffn_w_down, 1)])
    x2d, _ = _ffn(x2d, w_gate1, w_up1, w_down1, vec(ln_ffn_g[1]), vec(ln_ffn_b[1]))
    return x2d.reshape(bsz, seq, d)
```

```python
import functools
import math
import struct

import jax
import jax.numpy as jnp
from jax import lax
from jax.experimental import pallas as pl
from jax.experimental.pallas import tpu as pltpu

D_MODEL = 1024
DEPTH = 2
CHUNK = 64
N_HEADS = 8
HEAD_DIM_QK = 64
HEAD_DIM_V = 128
GMLP_BLOCK = 128
GMLP_HALF = 2 * D_MODEL
GMLP_GROUPS = 8
GMLP_GROUP_DIM = GMLP_HALF // GMLP_GROUPS
D_FF = 2816
DEEPNORM_ALPHA = (2 * DEPTH) ** 0.25
LN_EPS = 1e-5
MASK_VALUE = -1e30
QK_SCALE = HEAD_DIM_QK ** -0.5
LOG2E = math.log2(math.e)

V7X_VMEM_LIMIT_BYTES = 56 * 1024 * 1024
BF16_SUBLANES = 16

PROJ_TM = 1024
FFN_TM = 1024
FFN_SPLIT = 4
GMLP_TM = 256
ATTN_T = 256
KEY_BLOCK = 512
ATTN_HEADS_PER_STEP = 2
SCORE_SLOTS = 2

_BF16 = jnp.bfloat16
_F32 = jnp.float32


def _bf16_round(x):
    bits = struct.unpack("<I", struct.pack("<f", x))[0]
    bits = (bits + 0x7FFF + ((bits >> 16) & 1)) & 0xFFFF0000
    return struct.unpack("<f", struct.pack("<I", bits))[0]


def _bf16_pieces(x, n=3):
    pieces = []
    for _ in range(n):
        p = _bf16_round(x)
        pieces.append(p)
        x -= p
    return pieces


def _lambda_init(layer_idx):
    return 0.8 - 0.6 * math.exp(-0.3 * layer_idx)


def _layernorm(y, g, b):
    mu = jnp.mean(y, axis=-1, keepdims=True)
    d = y - mu
    var = jnp.mean(d * d, axis=-1, keepdims=True)
    return d * lax.rsqrt(var + LN_EPS) * g + b


def _dot(a, b):
    return jnp.dot(a, b, preferred_element_type=_F32)


def _dot_nt(a, b):
    return lax.dot_general(a, b, (((1,), (1,)), ((), ())), preferred_element_type=_F32)


def _resident(shape):
    return pl.BlockSpec(shape, lambda *_: (0,) * len(shape), pipeline_mode=pl.Buffered(1))


def _params(n_parallel_axes=1, flags=None):
    return pltpu.CompilerParams(
        dimension_semantics=("arbitrary",) * n_parallel_axes,
        vmem_limit_bytes=V7X_VMEM_LIMIT_BYTES,
        flags=flags,
    )


def _cast_plan(weights, steps, step_of):
    in_specs, out_specs, out_shapes = [], [], []
    for w, layer in weights:
        _, rows, cols = w.shape
        slices = steps
        while rows % (slices * BF16_SUBLANES):
            assert slices % 2 == 0, (w.shape, steps)
            slices //= 2
        hold = steps // slices
        block = (1, rows // slices, cols)
        in_specs.append(pl.BlockSpec(
            block, lambda *g, layer=layer, hold=hold: (layer, step_of(*g) // hold, 0)))
        out_specs.append(pl.BlockSpec(block, lambda *g, hold=hold: (0, step_of(*g) // hold, 0)))
        out_shapes.append(jax.ShapeDtypeStruct((1, rows, cols), _BF16))
    return in_specs, out_specs, out_shapes


def _run_casts(src_refs, dst_refs):
    for src, dst in zip(src_refs, dst_refs):
        dst[...] = src[...].astype(_BF16)


def _qkv_kernel(x_ref, wqt_ref, wk_ref, wvt_ref, *refs, n_cast):
    cast_src = refs[:n_cast]
    qt_ref, k_ref, vt_ref = refs[n_cast:n_cast + 3]
    cast_dst = refs[n_cast + 3:]
    xb = x_ref[...].astype(_BF16)
    qt_ref[0] = (_dot_nt(wqt_ref[...], xb) * (QK_SCALE * LOG2E)).astype(_BF16)
    k_ref[...] = _dot(xb, wk_ref[...]).astype(_BF16)
    vt_ref[0] = _dot_nt(wvt_ref[...], xb).astype(_BF16)
    _run_casts(cast_src, cast_dst)


def _qkv_proj(x2d, w_q_t, w_k, w_v_t, bsz, seq, cast_weights):
    m = x2d.shape[0]
    steps = m // PROJ_TM
    tiles_per_seq = seq // PROJ_TM
    row_spec = pl.BlockSpec((PROJ_TM, D_MODEL), lambda i: (i, 0))
    t_spec = pl.BlockSpec((1, D_MODEL, PROJ_TM),
                          lambda i: (i // tiles_per_seq, 0, i % tiles_per_seq))
    w_spec = _resident((D_MODEL, D_MODEL))
    t_shape = jax.ShapeDtypeStruct((bsz, D_MODEL, seq), _BF16)
    cast_in, cast_out, cast_shapes = _cast_plan(cast_weights, steps, lambda i: i)

    outs = pl.pallas_call(
        functools.partial(_qkv_kernel, n_cast=len(cast_weights)),
        grid=(steps,),
        in_specs=[row_spec, w_spec, w_spec, w_spec] + cast_in,
        out_specs=[t_spec, row_spec, t_spec] + cast_out,
        out_shape=[t_shape, jax.ShapeDtypeStruct((m, D_MODEL), _BF16), t_shape] + cast_shapes,
        compiler_params=_params(),
        name="qkv_proj",
    )(x2d, w_q_t, w_k, w_v_t, *[w for w, _ in cast_weights])
    return outs[0], outs[1], outs[2], outs[3:]


_LOG2E_PIECES = _bf16_pieces(LOG2E)


def _attn_kernel(slopes_ref, lam_ref, qt_ref, k_ref, vt_ref, g_ref, o_ref,
                 kaug_ref, qs_ref, s_ref, vones_ref, *, lambda_init):
    T = ATTN_T
    DQ = 2 * HEAD_DIM_QK
    DV = HEAD_DIM_V
    G = ATTN_HEADS_PER_STEP
    seq = k_ref.shape[1]
    n_tiles = seq // T
    slopes = [slopes_ref[pl.program_id(1) * G + g] for g in range(G)]

    NP = len(_LOG2E_PIECES)
    pos = lax.broadcasted_iota(jnp.int32, (seq, DQ), 0)
    lane_k = lax.broadcasted_iota(jnp.int32, (seq, DQ), 1)
    j_hi = jnp.where(lane_k < NP, (pos // CHUNK).astype(_F32), 0.0)
    j_lo = jnp.where((lane_k >= NP) & (lane_k < 2 * NP), (pos % CHUNK).astype(_F32), 0.0)
    for g in range(G):
        vones_ref[g, 0:DV, :] = vt_ref[0, g * DV:(g + 1) * DV, :]
        vones_ref[g, DV:DV + BF16_SUBLANES, :] = jnp.ones((BF16_SUBLANES, seq), _BF16)
        kaug_ref[g, :, 0:DQ] = k_ref[0, :, g * DQ:(g + 1) * DQ]
        kaug_ref[g, :, DQ:2 * DQ] = ((j_hi + j_lo) * slopes[g]).astype(_BF16)

    row_q = lax.broadcasted_iota(jnp.int32, (DQ, 2 * T), 0)
    query_aug = jnp.zeros((DQ, 2 * T), _F32)
    for idx, piece in enumerate(_LOG2E_PIECES):
        query_aug = jnp.where(row_q == idx, CHUNK * piece, query_aug)
        query_aug = jnp.where(row_q == idx + NP, piece, query_aug)
    query_aug = query_aug.astype(_BF16)

    lp = lam_ref[...]
    lam = (jnp.exp(jnp.sum(lp[0:1] * lp[1:2], axis=-1, keepdims=True))
           - jnp.exp(jnp.sum(lp[2:3] * lp[3:4], axis=-1, keepdims=True))
           + lambda_init)

    key_i = lax.broadcasted_iota(jnp.int32, (T, T), 0)
    qry_i = lax.broadcasted_iota(jnp.int32, (T, T), 1)
    ahead = jnp.maximum(key_i - qry_i, 0).astype(_F32)
    allowed = (key_i // CHUNK) <= (qry_i // CHUNK)

    diag_adds = []
    for g in range(G):
        d = jnp.where(allowed, (-2.0 * LOG2E * slopes[g]) * ahead, MASK_VALUE)
        diag_adds.append(jnp.concatenate([d, d], axis=1))

    row_d = lax.broadcasted_iota(jnp.int32, (DQ, T), 0)
    g_scaled = g_ref[...] * (1.0 - lambda_init)

    def stage_queries(g, t, slot):
        qt = qt_ref[0, g * DQ:(g + 1) * DQ, t * T:(t + 1) * T]
        zero = jnp.zeros_like(qt)
        top = jnp.concatenate([jnp.where(row_d < HEAD_DIM_QK, qt, zero),
                               jnp.where(row_d >= HEAD_DIM_QK, qt, zero)], axis=1)
        qs_ref[slot] = jnp.concatenate([top, query_aug], axis=0)

    def key_blocks(t):
        edges = list(range(0, t * T, KEY_BLOCK)) + [t * T]
        return [(lo, hi, False) for lo, hi in zip(edges[:-1], edges[1:])] + \
               [(t * T, (t + 1) * T, True)]

    def score_pass(g, t, slot, blk, m):
        lo, hi, is_diag = blk
        rows = slice(lo, hi)
        s = jnp.dot(kaug_ref[g, rows, :], qs_ref[slot],
                    preferred_element_type=_F32)
        if is_diag:
            s = s + diag_adds[g]
        s_ref[slot, rows, :] = s
        m_blk = jnp.max(s, axis=0, keepdims=True)
        return m_blk if m is None else jnp.maximum(m, m_blk)

    def value_pass(g, slot, blk, m, l, acc):
        rows = slice(blk[0], blk[1])
        p = jnp.exp2((s_ref[slot, rows, :] - m).astype(_BF16))
        acc_blk = jnp.dot(vones_ref[g, :, rows], p,
                          preferred_element_type=_F32)
        return None, (acc_blk if acc is None else acc + acc_blk)

    def finish(g, t, l, acc):
        att = acc[0:DV] / acc[DV:DV + 1]
        o = (att[:, 0:T] - lam * att[:, T:2 * T]).T
        ms = jnp.mean(o * o, axis=-1, keepdims=True)
        o = o * lax.rsqrt(ms + LN_EPS) * g_scaled
        o_ref[0, t * T:(t + 1) * T, g * DV:(g + 1) * DV] = o.astype(_BF16)

    work = []
    for g in range(G):
        tiles = range(n_tiles) if g % 2 == 0 else range(n_tiles - 1, -1, -1)
        work += [(g, t) for t in tiles]
    col_max = {}
    for i in range(len(work) + 1):
        if i < len(work):
            g, t = work[i]
            stage_queries(g, t, i % SCORE_SLOTS)
            m = None
            for blk in key_blocks(t):
                m = score_pass(g, t, i % SCORE_SLOTS, blk, m)
            col_max[i] = m
        if i > 0:
            g, t = work[i - 1]
            l, acc = None, None
            for blk in key_blocks(t):
                l, acc = value_pass(g, (i - 1) % SCORE_SLOTS, blk, col_max[i - 1], l, acc)
            finish(g, t, l, acc)


def _attention(qt, k, vt, slopes, lam_params, g_sub, lambda_init):
    b, s, _ = k.shape
    width = ATTN_HEADS_PER_STEP * HEAD_DIM_V
    head_spec = pl.BlockSpec((1, s, width), lambda bi, hi: (bi, 0, hi))
    head_t_spec = pl.BlockSpec((1, width, s), lambda bi, hi: (bi, hi, 0))
    return pl.pallas_call(
        functools.partial(_attn_kernel, lambda_init=lambda_init),
        grid=(b, N_HEADS // ATTN_HEADS_PER_STEP),
        in_specs=[
            pl.BlockSpec(memory_space=pltpu.SMEM),
            pl.BlockSpec((4, HEAD_DIM_QK), lambda bi, hi: (0, 0)),
            head_t_spec, head_spec, head_t_spec,
            pl.BlockSpec((1, HEAD_DIM_V), lambda bi, hi: (0, 0)),
        ],
        out_specs=head_spec,
        out_shape=jax.ShapeDtypeStruct((b, s, N_HEADS * HEAD_DIM_V), _BF16),
        scratch_shapes=[
            pltpu.VMEM((ATTN_HEADS_PER_STEP, s, 4 * HEAD_DIM_QK), _BF16),
            pltpu.VMEM((SCORE_SLOTS, 4 * HEAD_DIM_QK, 2 * ATTN_T), _BF16),
            pltpu.VMEM((SCORE_SLOTS, s, 2 * ATTN_T), _F32),
            pltpu.VMEM((ATTN_HEADS_PER_STEP, HEAD_DIM_V + BF16_SUBLANES, s), _BF16),
        ],
        compiler_params=_params(2),
        name="diff_attention",
    )(slopes, lam_params, qt, k, vt, g_sub)


def _ffn_kernel(*refs, with_out_proj, n_cast):
    if with_out_proj:
        o_ref, wo_ref, mix_g_ref, mix_b_ref = refs[:4]
        refs = refs[4:]
    x_ref, wg_ref, wu_ref, wd_ref, g_ref, b_ref = refs[:6]
    cast_src = refs[6:6 + n_cast]
    y_ref = refs[6 + n_cast]
    _run_casts(cast_src, refs[7 + n_cast:])
    rows_per = FFN_TM // FFN_SPLIT
    groups = [slice(i * rows_per, (i + 1) * rows_per) for i in range(FFN_SPLIT)]
    xs, acts = [], []
    for rows in groups:
        x = x_ref[rows, :]
        if with_out_proj:
            mix = _dot(o_ref[rows, :], wo_ref[0])
            x = _layernorm(DEEPNORM_ALPHA * x + mix, mix_g_ref[...], mix_b_ref[...])
        xs.append(x)
    for x in xs:
        xb = x.astype(_BF16)
        gate = _dot(xb, wg_ref[0])
        up = _dot(xb, wu_ref[0])
        acts.append((gate * jax.nn.sigmoid(gate) * up).astype(_BF16))
    for rows, x, act in zip(groups, xs, acts):
        y = _dot(act, wd_ref[0])
        y_ref[rows, :] = _layernorm(DEEPNORM_ALPHA * x + y, g_ref[...], b_ref[...])


def _ffn(x2d, w_gate, w_up, w_down, g, b, out_proj=None, cast_weights=()):
    m = x2d.shape[0]
    steps = m // FFN_TM
    row_spec = pl.BlockSpec((FFN_TM, D_MODEL), lambda i: (i, 0))
    vec_spec = pl.BlockSpec((1, D_MODEL), lambda i: (0, 0))
    in_specs = [row_spec, _resident((1, D_MODEL, D_FF)), _resident((1, D_MODEL, D_FF)),
                _resident((1, D_FF, D_MODEL)), vec_spec, vec_spec]
    args = [x2d, w_gate, w_up, w_down, g, b]
    if out_proj is not None:
        in_specs = [row_spec, _resident((1, D_MODEL, D_MODEL)), vec_spec, vec_spec] + in_specs
        args = list(out_proj) + args
    cast_in, cast_out, cast_shapes = _cast_plan(cast_weights, steps, lambda i: i)
    outs = pl.pallas_call(
        functools.partial(_ffn_kernel, with_out_proj=out_proj is not None,
                          n_cast=len(cast_weights)),
        grid=(steps,),
        in_specs=in_specs + cast_in,
        out_specs=[row_spec] + cast_out,
        out_shape=[jax.ShapeDtypeStruct((m, D_MODEL), _F32)] + cast_shapes,
        compiler_params=_params(),
        name="ffn_ln",
    )(*args, *[w for w, _ in cast_weights])
    return outs[0], outs[1:]


def _gmlp_kernel(x_ref, win_ref, bin_ref, lng_ref, lnb_ref, ws_ref, bs_ref,
                 wout_ref, g_ref, b_ref, *refs, n_cast):
    cast_src = refs[:n_cast]
    y_ref = refs[n_cast]
    cast_dst = refs[n_cast + 1:2 * n_cast + 1]
    za_ref, zb_ref, gated_a_ref, gated_b_ref = refs[2 * n_cast + 1:]
    _run_casts(cast_src, cast_dst)
    tm = GMLP_TM
    row = lax.broadcasted_iota(jnp.int32, (GMLP_BLOCK, GMLP_BLOCK), 0)
    col = lax.broadcasted_iota(jnp.int32, (GMLP_BLOCK, GMLP_BLOCK), 1)
    tri = (row >= col).astype(_F32)
    bs = bs_ref[...]

    def stage1(x, z_ref):
        z = _dot(x.astype(_BF16), win_ref[...]) + bin_ref[...]
        z_ref[...] = 0.5 * z * (1.0 + lax.erf(z * (2.0 ** -0.5)))

    def stage2a(z_ref, gated_ref):
        v = _layernorm(z_ref[:, GMLP_HALF:], lng_ref[...], lnb_ref[...]).astype(_BF16)
        for grp in range(GMLP_GROUPS):
            w_mix = (ws_ref[grp] * tri).astype(_BF16)
            cols = slice(grp * GMLP_GROUP_DIM, (grp + 1) * GMLP_GROUP_DIM)
            for blk in range(tm // GMLP_BLOCK):
                rows = slice(blk * GMLP_BLOCK, (blk + 1) * GMLP_BLOCK)
                mixed = _dot(w_mix, v[rows, cols]) + bs[:, grp:grp + 1]
                gated_ref[rows, cols] = (z_ref[rows, cols] * mixed).astype(_BF16)

    def stage2b(x, gated_ref, out_rows):
        y = _dot(gated_ref[...], wout_ref[...])
        y_ref[out_rows, :] = _layernorm(DEEPNORM_ALPHA * x + y, g_ref[...], b_ref[...])

    stage1(x_ref[0:tm, :], za_ref)
    stage1(x_ref[tm:2 * tm, :], zb_ref)
    stage2a(za_ref, gated_a_ref)
    stage2b(x_ref[0:tm, :], gated_a_ref, slice(0, tm))
    stage2a(zb_ref, gated_b_ref)
    stage2b(x_ref[tm:2 * tm, :], gated_b_ref, slice(tm, 2 * tm))


def _gmlp(x2d, w_in, b_in, ln_g, ln_b, w_s, b_s_t, w_out, g, b, cast_weights=()):
    m = x2d.shape[0]
    steps = m // (2 * GMLP_TM)
    pair_spec = pl.BlockSpec((2 * GMLP_TM, D_MODEL), lambda i: (i, 0))
    vec_spec = pl.BlockSpec((1, D_MODEL), lambda i: (0, 0))
    half_spec = pl.BlockSpec((1, GMLP_HALF), lambda i: (0, 0))
    cast_in, cast_out, cast_shapes = _cast_plan(cast_weights, steps, lambda i: i)
    outs = pl.pallas_call(
        functools.partial(_gmlp_kernel, n_cast=len(cast_weights)),
        grid=(steps,),
        in_specs=[pair_spec,
                  _resident((D_MODEL, 2 * GMLP_HALF)),
                  pl.BlockSpec((1, 2 * GMLP_HALF), lambda i: (0, 0)),
                  half_spec, half_spec,
                  _resident((GMLP_GROUPS, GMLP_BLOCK, GMLP_BLOCK)),
                  pl.BlockSpec((GMLP_BLOCK, GMLP_GROUPS), lambda i: (0, 0)),
                  _resident((GMLP_HALF, D_MODEL)),
                  vec_spec, vec_spec] + cast_in,
        out_specs=[pair_spec] + cast_out,
        out_shape=[jax.ShapeDtypeStruct((m, D_MODEL), _F32)] + cast_shapes,
        scratch_shapes=[pltpu.VMEM((GMLP_TM, 2 * GMLP_HALF), _F32),
                        pltpu.VMEM((GMLP_TM, 2 * GMLP_HALF), _F32),
                        pltpu.VMEM((GMLP_TM, GMLP_HALF), _BF16),
                        pltpu.VMEM((GMLP_TM, GMLP_HALF), _BF16)],
        compiler_params=_params(),
        name="gmlp_ln",
    )(x2d, w_in, b_in, ln_g, ln_b, w_s, b_s_t, w_out, g, b, *[w for w, _ in cast_weights])
    return outs[0], outs[1:]


def kernel(x, attn_w_qkv, attn_lambda_q1, attn_lambda_k1, attn_lambda_q2, attn_lambda_k2,
           attn_subln_g, attn_w_o, gmlp_w_in, gmlp_b_in, gmlp_ln_g, gmlp_ln_b, gmlp_w_s,
           gmlp_b_s, gmlp_w_out, ln_mix_g, ln_mix_b, ffn_w_gate, ffn_w_up, ffn_w_down,
           ln_ffn_g, ln_ffn_b):
    bsz, seq, d = x.shape
    assert d == D_MODEL and seq % PROJ_TM == 0 and seq % ATTN_T == 0
    m = bsz * seq
    x2d = x.reshape(m, d)
    slopes = 2.0 ** (-8.0 * jnp.arange(1, N_HEADS + 1, dtype=_F32) / N_HEADS)

    def vec(a):
        return a.reshape(1, -1)

    w_qkv = attn_w_qkv[0]
    qt, k, vt, (w_o, w_gate0, w_up0, w_down0) = _qkv_proj(
        x2d, w_qkv[:, :D_MODEL].T.astype(_BF16), w_qkv[:, D_MODEL:2 * D_MODEL].astype(_BF16),
        w_qkv[:, 2 * D_MODEL:].T.astype(_BF16), bsz, seq,
        [(attn_w_o, 0), (ffn_w_gate, 0), (ffn_w_up, 0), (ffn_w_down, 0)])
    lam_params = jnp.stack([attn_lambda_q1[0], attn_lambda_k1[0],
                            attn_lambda_q2[0], attn_lambda_k2[0]])
    o = _attention(qt, k.reshape(bsz, seq, d), vt, slopes, lam_params,
                   vec(attn_subln_g[0]), _lambda_init(0))
    x2d, (w_in, w_out) = _ffn(
        x2d, w_gate0, w_up0, w_down0, vec(ln_ffn_g[0]), vec(ln_ffn_b[0]),
        out_proj=(o.reshape(m, d), w_o, vec(ln_mix_g[0]), vec(ln_mix_b[0])),
        cast_weights=[(gmlp_w_in, 0), (gmlp_w_out, 0)])

    x2d, (w_gate1, w_up1, w_down1) = _gmlp(
        x2d, w_in[0], vec(gmlp_b_in[0]), vec(gmlp_ln_g[0]), vec(gmlp_ln_b[0]), gmlp_w_s[0],
        gmlp_b_s[0].T, w_out[0], vec(ln_mix_g[1]), vec(ln_mix_b[1]),
        cast_weights=[(ffn_w_gate, 1), (ffn_w_up, 1), (ffn_w_down, 1)])
    x2d, _ = _ffn(x2d, w_gate1, w_up1, w_down1, vec(ln_ffn_g[1]), vec(ln_ffn_b[1]))
    return x2d.reshape(bsz, seq, d)
```
